```python
import math
import jax
import jax.numpy as jnp
from jax import lax
import numpy as np

D_MODEL = 2048
BATCH = 2
SEQ = 8192
DEPTH = 1

D_SSM = D_MODEL // 2
SSM_GROUP = 16
N_SSM_GROUPS = D_SSM // SSM_GROUP
SSM_STATE = 64
D_ATT = D_MODEL - D_SSM
N_HEADS = 16
N_KV_HEADS = 4
HEAD_DIM = D_ATT // N_HEADS
GQA = N_HEADS // N_KV_HEADS
KV_DIM = N_KV_HEADS * HEAD_DIM
ROT_DIM = HEAD_DIM // 4
ROPE_THETA = 500000.0
CMP_BLOCK = 32
CMP_STRIDE = 16
CMP_HIDDEN = 2 * HEAD_DIM
SEL_BLOCK = 64
N_SELECT = 16
WINDOW = 512
Q_BLOCK = 128
D_FF = 5632
CONV_WIDTH = 3
NORM_EPS = 1e-6
NEG_BIG = -1e30
DT_MIN = 1e-3
DT_MAX = 1e-1
IN_SPLITS = (D_SSM, D_ATT, KV_DIM, KV_DIM, KV_DIM, KV_DIM, KV_DIM, KV_DIM, 3 * N_HEADS)
D_IN = sum(IN_SPLITS)

kernel_name = 'hymba_s5_nsa_convffn_block'


def rmsnorm(x, g):
    xf = x.astype(jnp.float32)
    y = xf * lax.rsqrt(jnp.mean(xf * xf, axis=-1, keepdims=True) + NORM_EPS)
    return (y * g.astype(jnp.float32)).astype(x.dtype)


def modulate(h, shift, scale):
    return h * (1.0 + scale[:, None, :]) + shift[:, None, :]


def partial_rope(x, pos):
    half = ROT_DIM // 2
    inv = ROPE_THETA ** (-jnp.arange(half, dtype=jnp.float32) / half)
    ang = pos.astype(jnp.float32)[:, None] * inv[None, :]
    cos = jnp.cos(ang)[None, :, None, :]
    sin = jnp.sin(ang)[None, :, None, :]
    xr = x[..., :ROT_DIM].astype(jnp.float32)
    x1, x2 = xr[..., :half], xr[..., half:]
    rot = jnp.concatenate([x1 * cos - x2 * sin, x1 * sin + x2 * cos], axis=-1).astype(x.dtype)
    return jnp.concatenate([rot, x[..., ROT_DIM:]], axis=-1)


def _lin_rec_combine(left, right):
    a_l, b_l = left
    a_r, b_r = right
    return a_r * a_l, a_r * b_l + b_r


def s5_mixer(u, lam_re, lam_im, log_dt, b_re, b_im, c_re, c_im, d_skip, w_glu, b_glu):
    bsz, seq, _ = u.shape
    f32 = jnp.float32
    uf = u.astype(f32).reshape(bsz, seq, N_SSM_GROUPS, SSM_GROUP)
    lr = lam_re.astype(f32)
    li = lam_im.astype(f32)
    dt = jnp.exp(log_dt.astype(f32))[:, None]
    mag = jnp.exp(lr * dt)
    ab_re = mag * jnp.cos(li * dt)
    ab_im = mag * jnp.sin(li * dt)
    den = lr * lr + li * li
    num_re = ab_re - 1.0
    f_re = (num_re * lr + ab_im * li) / den
    f_im = (ab_im * lr - num_re * li) / den
    br = b_re.astype(f32)
    bi = b_im.astype(f32)
    bb_re = f_re[..., None] * br - f_im[..., None] * bi
    bb_im = f_re[..., None] * bi + f_im[..., None] * br
    bu = lax.complex(jnp.einsum('bsgh,gph->sbgp', uf, bb_re),
                     jnp.einsum('bsgh,gph->sbgp', uf, bb_im))
    a = jnp.broadcast_to(lax.complex(ab_re, ab_im)[None, None], (seq, 1, N_SSM_GROUPS, SSM_STATE))
    _, states = lax.associative_scan(_lin_rec_combine, (a, bu), axis=0)
    y = (jnp.einsum('sbgp,ghp->bsgh', jnp.real(states), c_re.astype(f32))
         - jnp.einsum('sbgp,ghp->bsgh', jnp.imag(states), c_im.astype(f32))
         + d_skip.astype(f32) * uf)
    y = y.reshape(bsz, seq, D_SSM)
    z = jax.nn.gelu(y)
    out = z * jax.nn.sigmoid(z @ w_glu.astype(f32) + b_glu.astype(f32))
    return out.astype(u.dtype)


def compress_tokens(kv, pe, w1, w2):
    bsz, seq = kv.shape[:2]
    n_cmp = (seq - CMP_BLOCK) // CMP_STRIDE + 1
    idx = jnp.arange(n_cmp)[:, None] * CMP_STRIDE + jnp.arange(CMP_BLOCK)[None, :]
    blocks = kv[:, idx] + pe[None, None, :, None, :]
    flat = blocks.transpose(0, 1, 3, 2, 4).reshape(bsz, n_cmp, N_KV_HEADS, CMP_BLOCK * HEAD_DIM)
    return jax.nn.gelu(flat @ w1) @ w2


def nsa_mixer(q, kc_tok, vc_tok, ks, vs, kw, vw, gate_logits, pe_k, w1_k, w2_k, pe_v, w1_v, w2_v):
    f32 = jnp.float32
    bsz, seq = q.shape[:2]
    dtype = q.dtype
    n_cmp = (seq - CMP_BLOCK) // CMP_STRIDE + 1
    n_sel_blocks = seq // SEL_BLOCK
    n_sel = min(N_SELECT, n_sel_blocks)
    ratio = SEL_BLOCK // CMP_STRIDE
    n_overlap = CMP_BLOCK // CMP_STRIDE
    pad_sel = n_sel_blocks * ratio + n_overlap - 1 - n_cmp
    scale = HEAD_DIM ** -0.5

    kc = compress_tokens(kc_tok, pe_k, w1_k, w2_k)
    vc = compress_tokens(vc_tok, pe_v, w1_v, w2_v)
    cmp_end = jnp.arange(n_cmp) * CMP_STRIDE + CMP_BLOCK - 1
    ksb = ks.reshape(bsz, n_sel_blocks, SEL_BLOCK, N_KV_HEADS, HEAD_DIM).transpose(0, 3, 1, 2, 4)
    vsb = vs.reshape(bsz, n_sel_blocks, SEL_BLOCK, N_KV_HEADS, HEAD_DIM).transpose(0, 3, 1, 2, 4)
    kw_pad = jnp.pad(kw, ((0, 0), (WINDOW, 0), (0, 0), (0, 0)))
    vw_pad = jnp.pad(vw, ((0, 0), (WINDOW, 0), (0, 0), (0, 0)))
    gates = jax.nn.sigmoid(gate_logits.astype(f32)).reshape(bsz, seq, N_KV_HEADS, GQA, 3)
    qg = q.reshape(bsz, seq, N_KV_HEADS, GQA, HEAD_DIM)
    gather_blocks = jax.vmap(jax.vmap(lambda kb, ib: kb[ib]))
    blk = jnp.arange(n_sel_blocks)

    def query_block(i):
        start = i * Q_BLOCK
        t = start + jnp.arange(Q_BLOCK)
        qb = lax.dynamic_slice_in_dim(qg, start, Q_BLOCK, axis=1)
        gb = lax.dynamic_slice_in_dim(gates, start, Q_BLOCK, axis=1)
        s = jnp.einsum('bqhgd,bnhd->bhgqn', qb, kc).astype(f32) * scale
        m = cmp_end[None, :] <= t[:, None]
        p = jax.nn.softmax(jnp.where(m, s, NEG_BIG), axis=-1) * m
        o_cmp = jnp.einsum('bhgqn,bnhd->bqhgd', p.astype(dtype), vc)
        imp = jnp.pad(p.sum(axis=2), ((0, 0), (0, 0), (0, 0), (0, pad_sel)))
        imp = sum(imp[..., o:o + n_sel_blocks * ratio] for o in range(n_overlap))
        imp = imp.reshape(bsz, N_KV_HEADS, Q_BLOCK, n_sel_blocks, ratio).sum(-1)
        cur = t // SEL_BLOCK
        valid = blk[None, :] * SEL_BLOCK <= t[:, None]
        forced = (blk[None, :] == 0) | (blk[None, :] == cur[:, None]) | (blk[None, :] == cur[:, None] - 1)
        imp = jnp.where(forced, -NEG_BIG, jnp.where(valid, imp, NEG_BIG))
        _, sel = lax.top_k(imp, n_sel)
        k_sel = gather_blocks(ksb, sel)
        v_sel = gather_blocks(vsb, sel)
        s = jnp.einsum('bqhgd,bhqnsd->bhgqns', qb, k_sel).astype(f32) * scale
        pos = sel[..., None] * SEL_BLOCK + jnp.arange(SEL_BLOCK)
        m = (pos <= t[None, None, :, None, None])[:, :, None]
        s = jnp.where(m, s, NEG_BIG).reshape(bsz, N_KV_HEADS, GQA, Q_BLOCK, n_sel * SEL_BLOCK)
        p = jax.nn.softmax(s, axis=-1).reshape(bsz, N_KV_HEADS, GQA, Q_BLOCK, n_sel, SEL_BLOCK)
        o_sel = jnp.einsum('bhgqns,bhqnsd->bqhgd', p.astype(dtype), v_sel)
        kwb = lax.dynamic_slice_in_dim(kw_pad, start, WINDOW + Q_BLOCK, axis=1)
        vwb = lax.dynamic_slice_in_dim(vw_pad, start, WINDOW + Q_BLOCK, axis=1)
        kp = start - WINDOW + jnp.arange(WINDOW + Q_BLOCK)
        m = (kp[None, :] <= t[:, None]) & (kp[None, :] > t[:, None] - WINDOW) & (kp[None, :] >= 0)
        s = jnp.einsum('bqhgd,bkhd->bhgqk', qb, kwb).astype(f32) * scale
        p = jax.nn.softmax(jnp.where(m, s, NEG_BIG), axis=-1)
        o_win = jnp.einsum('bhgqk,bkhd->bqhgd', p.astype(dtype), vwb)
        o = gb[..., 0:1] * o_cmp + gb[..., 1:2] * o_sel + gb[..., 2:3] * o_win
        return o.astype(dtype)

    out = lax.map(query_block, jnp.arange(seq // Q_BLOCK))
    return out.transpose(1, 0, 2, 3, 4, 5).reshape(bsz, seq, D_ATT)


def conv_ffn(h, w_up, conv_w, conv_b, w_down):
    up = h @ w_up
    ch = up.shape[-1]
    up = lax.conv_general_dilated(up, conv_w[:, None, :].astype(up.dtype), window_strides=(1,),
                                  padding=[(CONV_WIDTH - 1, 0)],
                                  dimension_numbers=('NWC', 'WIO', 'NWC'),
                                  feature_group_count=ch) + conv_b
    val, gate = jnp.split(up, 2, axis=-1)
    return (jax.nn.silu(gate) * val) @ w_down


def setup_inputs(seed: int = 0) -> dict:
    key = jax.random.key(seed)
    keys = iter(jax.random.split(key, 40))
    f32 = jnp.float32

    def nrm(shape, s):
        return jax.random.normal(next(keys), shape, f32) * s

    L = DEPTH
    G, P, H = N_SSM_GROUPS, SSM_STATE, SSM_GROUP
    inp = {}
    inp['x'] = nrm((BATCH, SEQ, D_MODEL), 1.0)
    inp['c'] = nrm((BATCH, D_MODEL), 1.0)
    inp['w_ada'] = nrm((L, D_MODEL, 6 * D_MODEL), 0.5 * D_MODEL ** -0.5)
    inp['b_ada'] = nrm((L, 6 * D_MODEL), 0.02)
    inp['g_mix_norm'] = 1.0 + nrm((L, D_MODEL), 0.02)
    inp['w_in'] = nrm((L, D_MODEL, D_IN), D_MODEL ** -0.5)
    inp['lam_re'] = -0.5 + nrm((L, G, P), 0.01)
    inp['lam_im'] = jnp.broadcast_to(jnp.pi * jnp.arange(P, dtype=f32), (L, G, P)) + nrm((L, G, P), 0.001)
    inp['log_dt'] = jax.random.uniform(next(keys), (L, G), f32, math.log(DT_MIN), math.log(DT_MAX))
    inp['b_re'] = nrm((L, G, P, H), (2 * H) ** -0.5)
    inp['b_im'] = nrm((L, G, P, H), (2 * H) ** -0.5)
    inp['c_re'] = nrm((L, G, H, P), 0.5)
    inp['c_im'] = nrm((L, G, H, P), 0.5)
    inp['d_skip'] = nrm((L, G, H), 1.0)
    inp['w_glu'] = nrm((L, D_SSM, D_SSM), D_SSM ** -0.5)
    inp['b_glu'] = nrm((L, D_SSM), 0.02)
    inp['pe_k'] = nrm((L, CMP_BLOCK, HEAD_DIM), 0.1)
    inp['w1_k'] = nrm((L, CMP_BLOCK * HEAD_DIM, CMP_HIDDEN), (CMP_BLOCK * HEAD_DIM) ** -0.5)
    inp['w2_k'] = nrm((L, CMP_HIDDEN, HEAD_DIM), CMP_HIDDEN ** -0.5)
    inp['pe_v'] = nrm((L, CMP_BLOCK, HEAD_DIM), 0.1)
    inp['w1_v'] = nrm((L, CMP_BLOCK * HEAD_DIM, CMP_HIDDEN), (CMP_BLOCK * HEAD_DIM) ** -0.5)
    inp['w2_v'] = nrm((L, CMP_HIDDEN, HEAD_DIM), CMP_HIDDEN ** -0.5)
    inp['g_ssm_out'] = 1.0 + nrm((L, D_SSM), 0.02)
    inp['g_nsa_out'] = 1.0 + nrm((L, D_ATT), 0.02)
    inp['w_out'] = nrm((L, D_MODEL, D_MODEL), D_MODEL ** -0.5)
    inp['g_ffn_norm'] = 1.0 + nrm((L, D_MODEL), 0.02)
    inp['w_up'] = nrm((L, D_MODEL, 2 * D_FF), D_MODEL ** -0.5)
    inp['conv_w'] = nrm((L, CONV_WIDTH, 2 * D_FF), CONV_WIDTH ** -0.5)
    inp['conv_b'] = nrm((L, 2 * D_FF), 0.02)
    inp['w_down'] = nrm((L, D_FF, D_MODEL), D_FF ** -0.5)
    inp['g_final'] = 1.0 + nrm((D_MODEL,), 0.02)
    return inp


def reference(x, c, w_ada, b_ada, g_mix_norm, w_in, lam_re, lam_im, log_dt, b_re, b_im, c_re, c_im,
              d_skip, w_glu, b_glu, pe_k, w1_k, w2_k, pe_v, w1_v, w2_v, g_ssm_out, g_nsa_out, w_out,
              g_ffn_norm, w_up, conv_w, conv_b, w_down, g_final):
    bsz, seq, _ = x.shape
    pos = jnp.arange(seq)
    split_at = [int(v) for v in np.cumsum(IN_SPLITS)[:-1]]
    for l in range(DEPTH):
        mod = jax.nn.silu(c) @ w_ada[l] + b_ada[l]
        sh1, sc1, ga1, sh2, sc2, ga2 = jnp.split(mod, 6, axis=-1)
        h = modulate(rmsnorm(x, g_mix_norm[l]), sh1, sc1)
        proj = h @ w_in[l]
        u, q, kc, vc, ks, vs, kw, vw, gl = jnp.split(proj, split_at, axis=-1)
        y_ssm = s5_mixer(u, lam_re[l], lam_im[l], log_dt[l], b_re[l], b_im[l], c_re[l], c_im[l],
                         d_skip[l], w_glu[l], b_glu[l])
        kv_shape = (bsz, seq, N_KV_HEADS, HEAD_DIM)
        q = partial_rope(q.reshape(bsz, seq, N_HEADS, HEAD_DIM), pos)
        kc = partial_rope(kc.reshape(kv_shape), pos)
        ks = partial_rope(ks.reshape(kv_shape), pos)
        kw = partial_rope(kw.reshape(kv_shape), pos)
        y_att = nsa_mixer(q, kc, vc.reshape(kv_shape), ks, vs.reshape(kv_shape), kw, vw.reshape(kv_shape),
                          gl, pe_k[l], w1_k[l], w2_k[l], pe_v[l], w1_v[l], w2_v[l])
        y = jnp.concatenate([rmsnorm(y_ssm, g_ssm_out[l]), rmsnorm(y_att, g_nsa_out[l])], axis=-1) @ w_out[l]
        x = x + ga1[:, None, :] * y
        h = modulate(rmsnorm(x, g_ffn_norm[l]), sh2, sc2)
        x = x + ga2[:, None, :] * conv_ffn(h, w_up[l], conv_w[l], conv_b[l], w_down[l])
    return rmsnorm(x, g_final)
```

```python
import functools
import math

import jax
import jax.numpy as jnp
from jax import lax
from jax.experimental import pallas as pl
from jax.experimental.pallas import tpu as pltpu

F32 = jnp.float32
BF16 = jnp.bfloat16
HIGHEST = lax.Precision.HIGHEST

SSM_GROUP = 16
SSM_STATE = 64
N_HEADS = 16
N_KV_HEADS = 4
HEAD_DIM = 64
GQA = N_HEADS // N_KV_HEADS
KV_DIM = N_KV_HEADS * HEAD_DIM
ROT_DIM = HEAD_DIM // 4
ROPE_THETA = 500000.0
CMP_BLOCK = 32
CMP_STRIDE = 16
CMP_HIDDEN = 2 * HEAD_DIM
SEL_BLOCK = 64
N_SELECT = 16
WINDOW = 512
CONV_WIDTH = 3
NORM_EPS = 1e-6
NEG_BIG = -1e30

LANES = 128
S5_CHUNK = 16
S5_GROUPS_PER_STEP = 8
ATT_TILE = 256
VMEM_LIMIT = 56 * 2**20


def _params(sem, vmem=VMEM_LIMIT):
    return pltpu.CompilerParams(dimension_semantics=sem, vmem_limit_bytes=vmem)


def _const_spec(shape):
    n = len(shape)
    return pl.BlockSpec(shape, lambda *_: (0,) * n)


def _rms(x, g):
    return x * lax.rsqrt(jnp.mean(x * x, axis=-1, keepdims=True) + NORM_EPS) * g


def _dot(a, b):
    return jnp.dot(a, b, preferred_element_type=F32)


def _dot_nt(a, b, precision=None):
    return lax.dot_general(a, b, (((1,), (1,)), ((), ())), precision=precision,
                           preferred_element_type=F32)


def _mod_kernel(c_ref, w_ref, b_ref, o_ref):
    c = c_ref[...]
    sc = c * jax.nn.sigmoid(c)
    o_ref[...] = jnp.dot(sc, w_ref[...], preferred_element_type=F32, precision=HIGHEST) + b_ref[...]


def _adaln_mod(c, w_ada, b_ada):
    bsz, d = c.shape
    n = w_ada.shape[1]
    tn = n // 8
    cp = jnp.zeros((8, d), F32).at[:bsz].set(c)
    out = pl.pallas_call(
        _mod_kernel,
        grid=(n // tn,),
        in_specs=[pl.BlockSpec((8, d), lambda j: (0, 0)),
                  pl.BlockSpec((d, tn), lambda j: (0, j)),
                  pl.BlockSpec((1, tn), lambda j: (0, j))],
        out_specs=pl.BlockSpec((8, tn), lambda j: (0, j)),
        out_shape=jax.ShapeDtypeStruct((8, n), F32),
        compiler_params=_params(("arbitrary",)),
        name="adaln_mod",
    )(cp, w_ada, b_ada.reshape(1, n))
    return out[:bsz]


def _rope_kernel(c_ref, sm_ref, sp_ref, *, tr):
    i = pl.program_id(0)
    pos = (i * tr + lax.broadcasted_iota(jnp.int32, (tr, LANES), 0)).astype(F32)
    lane = lax.broadcasted_iota(jnp.int32, (tr, LANES), 1)
    d = lane & (HEAD_DIM - 1)
    half = ROT_DIM // 2
    fi = d & (half - 1)
    inv = jnp.zeros((tr, LANES), F32)
    for k in range(half):
        inv = jnp.where(fi == k, ROPE_THETA ** (-k / half), inv)
    ang = pos * inv
    cs = jnp.cos(ang)
    sn = jnp.sin(ang)
    c_ref[...] = jnp.where(d < ROT_DIM, cs, 1.0)
    sm_ref[...] = jnp.where(d < half, -sn, 0.0)
    sp_ref[...] = jnp.where((d >= half) & (d < ROT_DIM), sn, 0.0)


def _rope_tables(seq):
    tr = min(seq, 1024)
    sds = jax.ShapeDtypeStruct((seq, LANES), F32)
    spec = pl.BlockSpec((tr, LANES), lambda i: (i, 0))
    return pl.pallas_call(
        functools.partial(_rope_kernel, tr=tr),
        grid=(seq // tr,),
        in_specs=[],
        out_specs=[spec, spec, spec],
        out_shape=[sds, sds, sds],
        compiler_params=_params(("arbitrary",)),
        name="rope_tables",
    )()


def _proj_kernel(x_ref, g_ref, sh_ref, sc_ref, wu_ref, wq_ref, wkv_ref, wg_ref, rc_ref, rm_ref, rp_ref,
                 u_ref, q_ref, kc_ref, vc_ref, kst_ref, vs_ref, kwt_ref, vw_ref, gt_ref):
    x = x_ref[...]
    h = _rms(x, g_ref[...]) * (1.0 + sc_ref[0]) + sh_ref[0]
    hb = h.astype(BF16)
    rc, rm, rp = rc_ref[...], rm_ref[...], rp_ref[...]

    def rope(s):
        return s * rc + pltpu.roll(s, LANES - ROT_DIM // 2, 1) * rm + pltpu.roll(s, ROT_DIM // 2, 1) * rp

    def roped(a):
        return jnp.concatenate([rope(a[:, :LANES]), rope(a[:, LANES:])], axis=1)

    u_ref[...] = _dot(hb, wu_ref[...]).astype(BF16)

    q = _dot(hb, wq_ref[...])
    scale = HEAD_DIM ** -0.5
    for j in range(N_HEADS // 2):
        s = rope(q[:, j * LANES:(j + 1) * LANES]) * scale
        q_ref[0, 2 * j] = s[:, :HEAD_DIM].astype(BF16)
        q_ref[0, 2 * j + 1] = s[:, HEAD_DIM:].astype(BF16)

    kv = _dot(hb, wkv_ref[...])

    def seg(i):
        return kv[:, i * KV_DIM:(i + 1) * KV_DIM]

    kc_ref[...] = roped(seg(0)).astype(BF16)
    vc_ref[...] = seg(1).astype(BF16)
    kst_ref[0] = roped(seg(2)).T.astype(BF16)
    vs = seg(3)
    kwt_ref[0] = roped(seg(4)).T.astype(BF16)
    vw = seg(5)
    for hh in range(N_KV_HEADS):
        vs_ref[0, hh] = vs[:, hh * HEAD_DIM:(hh + 1) * HEAD_DIM].astype(BF16)
        vw_ref[0, hh] = vw[:, hh * HEAD_DIM:(hh + 1) * HEAD_DIM].astype(BF16)

    sg = jax.nn.sigmoid(_dot(hb, wg_ref[...]))
    gt_ref[0] = sg
    for hh in range(1, N_KV_HEADS):
        gt_ref[hh] = pltpu.roll(sg, LANES - 3 * GQA * hh, 1)


def _input_proj(x2, g, sh, sc, w_in, rope_tabs, bsz, seq, tm):
    t, d = x2.shape
    d_ssm = d // 2
    d_att = d - d_ssm
    nt = seq // tm
    w_in = w_in.astype(BF16)
    o = 0
    wu = w_in[:, o:o + d_ssm]; o += d_ssm
    wq = w_in[:, o:o + d_att]; o += d_att
    wkv = w_in[:, o:o + 6 * KV_DIM]; o += 6 * KV_DIM
    wg = jnp.zeros((d, LANES), BF16).at[:, :3 * N_HEADS].set(w_in[:, o:])
    row = lambda i: (i, 0)
    bat = lambda i: (i // nt, 0, 0)
    tab = lambda i: (i % nt, 0)
    hm = lambda i: (i // nt, 0, i % nt, 0)
    tr = lambda i: (i // nt, 0, i % nt)
    in_specs = [pl.BlockSpec((tm, d), row), _const_spec((1, d)),
                pl.BlockSpec((1, 1, d), bat), pl.BlockSpec((1, 1, d), bat),
                _const_spec(wu.shape), _const_spec(wq.shape), _const_spec(wkv.shape), _const_spec(wg.shape),
                pl.BlockSpec((tm, LANES), tab), pl.BlockSpec((tm, LANES), tab), pl.BlockSpec((tm, LANES), tab)]
    out_shape = [jax.ShapeDtypeStruct((t, d_ssm), BF16),
                 jax.ShapeDtypeStruct((bsz, N_HEADS, seq, HEAD_DIM), BF16),
                 jax.ShapeDtypeStruct((t, KV_DIM), BF16),
                 jax.ShapeDtypeStruct((t, KV_DIM), BF16),
                 jax.ShapeDtypeStruct((bsz, KV_DIM, seq), BF16),
                 jax.ShapeDtypeStruct((bsz, N_KV_HEADS, seq, HEAD_DIM), BF16),
                 jax.ShapeDtypeStruct((bsz, KV_DIM, seq), BF16),
                 jax.ShapeDtypeStruct((bsz, N_KV_HEADS, seq, HEAD_DIM), BF16),
                 jax.ShapeDtypeStruct((N_KV_HEADS, t, LANES), F32)]
    out_specs = [pl.BlockSpec((tm, d_ssm), row),
                 pl.BlockSpec((1, N_HEADS, tm, HEAD_DIM), hm),
                 pl.BlockSpec((tm, KV_DIM), row),
                 pl.BlockSpec((tm, KV_DIM), row),
                 pl.BlockSpec((1, KV_DIM, tm), tr),
                 pl.BlockSpec((1, N_KV_HEADS, tm, HEAD_DIM), hm),
                 pl.BlockSpec((1, KV_DIM, tm), tr),
                 pl.BlockSpec((1, N_KV_HEADS, tm, HEAD_DIM), hm),
                 pl.BlockSpec((N_KV_HEADS, tm, LANES), lambda i: (0, i, 0))]
    return pl.pallas_call(
        _proj_kernel,
        grid=(t // tm,),
        in_specs=in_specs, out_specs=out_specs, out_shape=out_shape,
        compiler_params=_params(("arbitrary",)),
        name="input_proj",
    )(x2, g.reshape(1, d), sh.reshape(bsz, 1, d), sc.reshape(bsz, 1, d), wu, wq, wkv, wg, *rope_tabs)


def _s5prep_kernel(lr_ref, li_ref, ldt_ref, btr_ref, bti_ref, cr_ref, ci_ref, dsk_ref,
                   m_ref, bzr_ref, bzi_ref, cyr_ref, cyi_ref, ar_ref, ai_ref):
    L, H, P = S5_CHUNK, SSM_GROUP, SSM_STATE
    lr = lr_ref[0]
    li = li_ref[0]
    dt = jnp.exp(ldt_ref[0])
    mag = jnp.exp(lr * dt)
    ab_re = mag * jnp.cos(li * dt)
    ab_im = mag * jnp.sin(li * dt)
    den = lr * lr + li * li
    num_re = ab_re - 1.0
    f_re = (num_re * lr + ab_im * li) / den
    f_im = (ab_im * lr - num_re * li) / den
    btr = btr_ref[0]
    bti = bti_ref[0]
    bb_re = f_re * btr - f_im * bti
    bb_im = f_re * bti + f_im * btr
    c_re = cr_ref[0]
    c_im = ci_ref[0]

    r = lax.broadcasted_iota(jnp.int32, (4 * L, 1), 0)
    k = r & (L - 1)
    blk = r >> 4
    e = jnp.where(blk == 0, -k, jnp.where(blk == 1, k, jnp.where(blk == 2, L - 1 - k, k + 1))).astype(F32)
    pm = jnp.exp(lr * dt * e)
    ph = li * dt * e
    pr = pm * jnp.cos(ph)
    pi = pm * jnp.sin(ph)

    def rows(i, sm_r, sm_i):
        a_r = jnp.broadcast_to(pr[i * L:(i + 1) * L][:, None, :], (L, H, P)).reshape(L * H, P)
        a_i = jnp.broadcast_to(pi[i * L:(i + 1) * L][:, None, :], (L, H, P)).reshape(L * H, P)
        s_r = jnp.broadcast_to(sm_r[None], (L, H, P)).reshape(L * H, P)
        s_i = jnp.broadcast_to(sm_i[None], (L, H, P)).reshape(L * H, P)
        return a_r * s_r - a_i * s_i, a_r * s_i + a_i * s_r

    x_re, x_im = rows(0, bb_re, bb_im)
    y_re, y_im = rows(1, c_re, c_im)
    z_re, z_im = rows(2, bb_re, bb_im)
    w_re, w_im = rows(3, c_re, c_im)
    mm = _dot_nt(x_re, y_re, HIGHEST) - _dot_nt(x_im, y_im, HIGHEST)
    ri = lax.broadcasted_iota(jnp.int32, (L * H, L * H), 0)
    ci = lax.broadcasted_iota(jnp.int32, (L * H, L * H), 1)
    mm = jnp.where((ci >> 4) >= (ri >> 4), mm, 0.0) + jnp.where(ri == ci, dsk_ref[0], 0.0)
    m_ref[0] = mm.astype(BF16)
    bzr_ref[0] = z_re.astype(BF16)
    bzi_ref[0] = z_im.astype(BF16)
    cyr_ref[0] = w_re.astype(BF16)
    cyi_ref[0] = (-w_im).astype(BF16)
    m16 = jnp.exp(lr * dt * L)
    ar_ref[0] = m16 * jnp.cos(li * dt * L)
    ai_ref[0] = m16 * jnp.sin(li * dt * L)


def _s5_prep(lam_re, lam_im, log_dt, b_re, b_im, c_re, c_im, d_skip):
    g, p, h = b_re.shape
    lh = S5_CHUNK * h
    g3 = lambda i: (i, 0, 0)
    ins = [lam_re.reshape(g, 1, p), lam_im.reshape(g, 1, p),
           jnp.broadcast_to(log_dt.reshape(g, 1, 1), (g, 1, p)),
           b_re.transpose(0, 2, 1), b_im.transpose(0, 2, 1), c_re, c_im,
           jnp.tile(d_skip, (1, S5_CHUNK)).reshape(g, 1, lh)]
    in_specs = [pl.BlockSpec((1,) + a.shape[1:], g3) for a in ins]
    out_shape = [jax.ShapeDtypeStruct((g, lh, lh), BF16)] + \
                [jax.ShapeDtypeStruct((g, lh, p), BF16)] * 4 + \
                [jax.ShapeDtypeStruct((g, 1, p), F32)] * 2
    out_specs = [pl.BlockSpec((1,) + s.shape[1:], g3) for s in out_shape]
    return pl.pallas_call(
        _s5prep_kernel, grid=(g,), in_specs=in_specs, out_specs=out_specs, out_shape=out_shape,
        compiler_params=_params(("arbitrary",)), name="s5_prep",
    )(*ins)


def _s5_kernel(u_ref, m_ref, bzr_ref, bzi_ref, cyr_ref, cyi_ref, ar_ref, ai_ref, y_ref,
               yi_sc, zr_sc, zi_sc, xr_sc, xi_sc, *, gb, nch, bsz):
    P = SSM_STATE
    for g in range(gb):
        ug = u_ref[g]
        yi_sc[g] = _dot(ug, m_ref[g])
        zr_sc[:, g * P:(g + 1) * P] = _dot(ug, bzr_ref[g])
        zi_sc[:, g * P:(g + 1) * P] = _dot(ug, bzi_ref[g])
    a_r = ar_ref[0]
    a_i = ai_ref[0]

    def body(c, carry):
        new = []
        for b in range(bsz):
            x_r, x_i = carry[2 * b], carry[2 * b + 1]
            row = b * nch + c
            xr_sc[pl.ds(row, 1), :] = x_r
            xi_sc[pl.ds(row, 1), :] = x_i
            z_r = zr_sc[pl.ds(row, 1), :]
            z_i = zi_sc[pl.ds(row, 1), :]
            new.append(a_r * x_r - a_i * x_i + z_r)
            new.append(a_r * x_i + a_i * x_r + z_i)
        return tuple(new)

    zero = jnp.zeros((1, gb * P), F32)
    lax.fori_loop(0, nch, body, (zero,) * (2 * bsz))
    for g in range(gb):
        x_r = xr_sc[:, g * P:(g + 1) * P].astype(BF16)
        x_i = xi_sc[:, g * P:(g + 1) * P].astype(BF16)
        y = yi_sc[g] + _dot_nt(x_r, cyr_ref[g]) + _dot_nt(x_i, cyi_ref[g])
        y_ref[g] = y.astype(BF16)


def _s5_scan(u_g, ops, bsz):
    m, bzr, bzi, cyr, cyi, a_r, a_i = ops
    g, rows, lh = u_g.shape
    p = SSM_STATE
    gb = S5_GROUPS_PER_STEP
    nch = rows // bsz
    a_r = a_r.reshape(g // gb, 1, gb * p)
    a_i = a_i.reshape(g // gb, 1, gb * p)
    g3 = lambda i: (i, 0, 0)
    in_specs = [pl.BlockSpec((gb, rows, lh), g3), pl.BlockSpec((gb, lh, lh), g3)] + \
               [pl.BlockSpec((gb, lh, p), g3)] * 4 + [pl.BlockSpec((1, 1, gb * p), g3)] * 2
    return pl.pallas_call(
        functools.partial(_s5_kernel, gb=gb, nch=nch, bsz=bsz),
        grid=(g // gb,),
        in_specs=in_specs,
        out_specs=pl.BlockSpec((gb, rows, lh), g3),
        out_shape=jax.ShapeDtypeStruct((g, rows, lh), BF16),
        scratch_shapes=[pltpu.VMEM((gb, rows, lh), F32)] + [pltpu.VMEM((rows, gb * p), F32)] * 4,
        compiler_params=_params(("arbitrary",)),
        name="s5_scan",
    )(u_g, m, bzr, bzi, cyr, cyi, a_r, a_i)


def _cmp_kernel(tok_ref, w1x_ref, pe_ref, w1_ref, w2x_ref, o_ref, *, transpose_out):
    nh = N_KV_HEADS * CMP_HIDDEN
    g = _dot(tok_ref[...], w1x_ref[...])
    rows = g.shape[0]
    bias = jnp.dot(jnp.broadcast_to(pe_ref[...], (8, pe_ref.shape[1])), w1_ref[...],
                   preferred_element_type=F32, precision=HIGHEST)[0:1]
    bias = jnp.concatenate([bias] * N_KV_HEADS, axis=1)
    hid = g[:, :nh] + pltpu.roll(g[:, nh:], rows - 1, 0) + bias
    out = _dot(jax.nn.gelu(hid).astype(BF16), w2x_ref[...])
    if transpose_out:
        o_ref[0] = out.T.astype(BF16)
    else:
        for hh in range(N_KV_HEADS):
            o_ref[0, hh] = out[:, hh * HEAD_DIM:(hh + 1) * HEAD_DIM].astype(BF16)


def _compress(tok, pe, w1, w2, bsz, seq, transpose_out):
    ncp = seq // CMP_STRIDE
    half = CMP_STRIDE * HEAD_DIM
    eye = jnp.eye(N_KV_HEADS, dtype=F32)
    w1r = w1.reshape(2, CMP_STRIDE, HEAD_DIM, CMP_HIDDEN)
    w1x = jnp.einsum('zkdj,hg->khdzgj', w1r, eye).reshape(CMP_STRIDE * KV_DIM, 2 * N_KV_HEADS * CMP_HIDDEN)
    w2x = jnp.einsum('jd,hg->hjgd', w2, eye).reshape(N_KV_HEADS * CMP_HIDDEN, KV_DIM)
    tok2 = tok.reshape(bsz * ncp, CMP_STRIDE * KV_DIM)
    if transpose_out:
        out_shape = jax.ShapeDtypeStruct((bsz, KV_DIM, ncp), BF16)
        out_spec = pl.BlockSpec((1, KV_DIM, ncp), lambda b: (b, 0, 0))
    else:
        out_shape = jax.ShapeDtypeStruct((bsz, N_KV_HEADS, ncp, HEAD_DIM), BF16)
        out_spec = pl.BlockSpec((1, N_KV_HEADS, ncp, HEAD_DIM), lambda b: (b, 0, 0, 0))
    return pl.pallas_call(
        functools.partial(_cmp_kernel, transpose_out=transpose_out),
        grid=(bsz,),
        in_specs=[pl.BlockSpec((ncp, CMP_STRIDE * KV_DIM), lambda b: (b, 0)),
                  _const_spec(w1x.shape), _const_spec((1, 2 * half)), _const_spec(w1.shape),
                  _const_spec(w2x.shape)],
        out_specs=out_spec, out_shape=out_shape,
        compiler_params=_params(("arbitrary",)),
        name="compress_k" if transpose_out else "compress_v",
    )(tok2, w1x.astype(BF16), pe.reshape(1, 2 * half), w1, w2x.astype(BF16))


def _nsa_kernel(q_ref, kct_ref, vc_ref, kst_ref, vs_ref, kwt_ref, vw_ref, gt_ref, amat_ref, o_ref,
                rhs_sc, lhs_sc, m_sc, l_sc, acc_sc, *, tq, seq):
    nblk = seq // SEL_BLOCK
    ncp = seq // CMP_STRIDE
    rows = GQA * tq
    qi = pl.program_id(2)
    t0 = qi * tq

    @pl.when(qi == 0)
    def _():
        rhs_sc[0:HEAD_DIM, :] = kst_ref[0]
        rhs_sc[HEAD_DIM:LANES, :] = jnp.zeros((LANES - HEAD_DIM, seq), BF16)
        cw = min(seq, 1024)

        def fill(c, _):
            col = c * cw + lax.broadcasted_iota(jnp.int32, (nblk, cw), 1)
            blk = lax.broadcasted_iota(jnp.int32, (nblk, cw), 0)
            onehot = jnp.where((col >> 6) == blk, 1.0, 0.0).astype(BF16)
            rhs_sc[LANES:LANES + nblk, pl.ds(pl.multiple_of(c * cw, cw), cw)] = onehot
            return 0

        lax.fori_loop(0, seq // cw, fill, 0)

    q = q_ref[0].reshape(rows, HEAD_DIM)
    t_row = t0 + (lax.broadcasted_iota(jnp.int32, (rows, 1), 0) & (tq - 1))

    s = _dot(q, kct_ref[0])
    n_col = lax.broadcasted_iota(jnp.int32, (1, ncp), 1)
    valid = (n_col * CMP_STRIDE + (CMP_BLOCK - 1)) <= t_row
    sm = jnp.where(valid, s, NEG_BIG)
    e = jnp.exp(sm - jnp.max(sm, axis=-1, keepdims=True))
    p = jnp.where(valid, e / jnp.sum(e, axis=-1, keepdims=True), 0.0)
    o_cmp = _dot(p.astype(BF16), vc_ref[0, 0])

    psum = p[0:tq]
    for g in range(1, GQA):
        psum = psum + p[g * tq:(g + 1) * tq]
    amat = amat_ref[...]
    p_hi = psum.astype(BF16)
    r1 = psum - p_hi.astype(F32)
    p_mid = r1.astype(BF16)
    p_lo = (r1 - p_mid.astype(F32)).astype(BF16)
    imp = _dot(p_hi, amat) + _dot(p_mid, amat) + _dot(p_lo, amat)
    t_q = t0 + lax.broadcasted_iota(jnp.int32, (tq, 1), 0)
    blk = lax.broadcasted_iota(jnp.int32, (1, nblk), 1)
    blk_f = blk.astype(F32)
    cur = t_q >> 6
    forced = (blk == 0) | (blk == cur) | (blk == cur - 1)
    impm = jnp.where(forced, -NEG_BIG, jnp.where(blk * SEL_BLOCK <= t_q, imp, NEG_BIG))
    selneg = jnp.full((tq, nblk), NEG_BIG, F32)
    for _ in range(min(N_SELECT, nblk)):
        best = jnp.max(impm, axis=-1, keepdims=True)
        idx = jnp.min(jnp.where(impm == best, blk_f, float(nblk)), axis=-1, keepdims=True)
        pick = blk_f == idx
        selneg = jnp.where(pick, 0.0, selneg)
        impm = jnp.where(pick, -jnp.inf, impm)
    selneg = selneg.astype(BF16)

    lhs_sc[:, 0:HEAD_DIM] = q
    lhs_sc[:, HEAD_DIM:LANES] = jnp.zeros((rows, LANES - HEAD_DIM), BF16)
    for g in range(GQA):
        lhs_sc[g * tq:(g + 1) * tq, LANES:LANES + nblk] = selneg

    def reset():
        m_sc[...] = jnp.full((rows, 1), NEG_BIG, F32)
        l_sc[...] = jnp.zeros((rows, 1), F32)
        acc_sc[...] = jnp.zeros((rows, HEAD_DIM), F32)

    def online(s, v):
        m_old = m_sc[...]
        m_new = jnp.maximum(m_old, jnp.max(s, axis=-1, keepdims=True))
        alpha = jnp.exp(m_old - m_new)
        pe = jnp.exp(s - m_new)
        l_sc[...] = alpha * l_sc[...] + jnp.sum(pe, axis=-1, keepdims=True)
        acc_sc[...] = alpha * acc_sc[...] + _dot(pe.astype(BF16), v)
        m_sc[...] = m_new

    def result():
        return acc_sc[...] / l_sc[...]

    def kpos(kj):
        return kj * tq + lax.broadcasted_iota(jnp.int32, (1, tq), 1)

    reset()

    def sel_step(kj, _):
        off = pl.multiple_of(kj * tq, tq)
        s = _dot(lhs_sc[...], rhs_sc[:, pl.ds(off, tq)])
        online(s, vs_ref[0, 0, pl.ds(off, tq), :])
        return 0

    lax.fori_loop(0, qi, sel_step, 0)
    off = pl.multiple_of(t0, tq)
    s = _dot(lhs_sc[...], rhs_sc[:, pl.ds(off, tq)])
    s = jnp.where(kpos(qi) <= t_row, s, NEG_BIG)
    online(s, vs_ref[0, 0, pl.ds(off, tq), :])
    o_sel = result()

    reset()

    def win_step(kj, _):
        off = pl.multiple_of(kj * tq, tq)
        s = _dot(q, kwt_ref[0, :, pl.ds(off, tq)])
        kp = kpos(kj)
        s = jnp.where((kp <= t_row) & (kp > t_row - WINDOW), s, NEG_BIG)
        online(s, vw_ref[0, 0, pl.ds(off, tq), :])
        return 0

    lax.fori_loop(jnp.maximum(qi - WINDOW // tq, 0), qi + 1, win_step, 0)
    o_win = result()

    gt = gt_ref[0]
    for g in range(GQA):
        sl = slice(g * tq, (g + 1) * tq)
        o = (gt[:, 3 * g:3 * g + 1] * o_cmp[sl] + gt[:, 3 * g + 1:3 * g + 2] * o_sel[sl]
             + gt[:, 3 * g + 2:3 * g + 3] * o_win[sl])
        o_ref[:, g * HEAD_DIM:(g + 1) * HEAD_DIM] = o


def _importance_matrix(seq):
    ncp = seq // CMP_STRIDE
    nblk = seq // SEL_BLOCK
    ratio = SEL_BLOCK // CMP_STRIDE
    n = jnp.arange(ncp)[:, None]
    j = jnp.arange(nblk)[None, :]
    a = sum(((n == ratio * j + r).astype(F32) + (n == ratio * j + r + 1).astype(F32)) for r in range(ratio))
    return jnp.where(n < ncp - 1, a, 0.0).astype(BF16)


def _nsa_attention(q_hm, kct, vc_hm, kst, vs_hm, kwt, vw_hm, gates, bsz, seq):
    tq = min(ATT_TILE, seq)
    nq = seq // tq
    nblk = seq // SEL_BLOCK
    ncp = seq // CMP_STRIDE
    rows = GQA * tq
    amat = _importance_matrix(seq)
    head = lambda b, h, i: (b, h, 0, 0)
    headt = lambda b, h, i: (b, h, 0)
    in_specs = [pl.BlockSpec((1, GQA, tq, HEAD_DIM), lambda b, h, i: (b, h, i, 0)),
                pl.BlockSpec((1, HEAD_DIM, ncp), headt),
                pl.BlockSpec((1, 1, ncp, HEAD_DIM), head),
                pl.BlockSpec((1, HEAD_DIM, seq), headt),
                pl.BlockSpec((1, 1, seq, HEAD_DIM), head),
                pl.BlockSpec((1, HEAD_DIM, seq), headt),
                pl.BlockSpec((1, 1, seq, HEAD_DIM), head),
                pl.BlockSpec((1, tq, LANES), lambda b, h, i: (h, b * nq + i, 0)),
                _const_spec(amat.shape)]
    return pl.pallas_call(
        functools.partial(_nsa_kernel, tq=tq, seq=seq),
        grid=(bsz, N_KV_HEADS, nq),
        in_specs=in_specs,
        out_specs=pl.BlockSpec((tq, GQA * HEAD_DIM), lambda b, h, i: (b * nq + i, h)),
        out_shape=jax.ShapeDtypeStruct((bsz * seq, N_HEADS * HEAD_DIM), F32),
        scratch_shapes=[pltpu.VMEM((LANES + nblk, seq), BF16),
                        pltpu.VMEM((rows, LANES + nblk), BF16),
                        pltpu.VMEM((rows, 1), F32), pltpu.VMEM((rows, 1), F32),
                        pltpu.VMEM((rows, HEAD_DIM), F32)],
        compiler_params=_params(("arbitrary", "arbitrary", "arbitrary")),
        name="nsa_attention",
    )(q_hm, kct, vc_hm, kst, vs_hm, kwt, vw_hm, gates, amat)


def _out_kernel(ys_ref, ya_ref, x_ref, wglu_ref, bglu_ref, gs_ref, ga_ref, woa_ref, wob_ref,
                ga1_ref, gf_ref, sh2_ref, sc2_ref, x1_ref, h2_ref):
    z = jax.nn.gelu(ys_ref[...].astype(F32))
    glu = z * jax.nn.sigmoid(_dot(z.astype(BF16), wglu_ref[...]) + bglu_ref[...])
    n1 = _rms(glu, gs_ref[...]).astype(BF16)
    n2 = _rms(ya_ref[...], ga_ref[...]).astype(BF16)
    y = _dot(n1, woa_ref[...]) + _dot(n2, wob_ref[...])
    x1 = x_ref[...] + ga1_ref[0] * y
    x1_ref[...] = x1
    h2_ref[...] = (_rms(x1, gf_ref[...]) * (1.0 + sc2_ref[0]) + sh2_ref[0]).astype(BF16)


def _out_proj(y_ssm, y_att, x2, w_glu, b_glu, g_ssm, g_nsa, w_out, ga1, g_ffn, sh2, sc2, bsz, seq, tm):
    t, d = x2.shape
    d_ssm = y_ssm.shape[1]
    d_att = y_att.shape[1]
    nt = seq // tm
    row = lambda i: (i, 0)
    bat = lambda i: (i // nt, 0, 0)
    w_out = w_out.astype(BF16)
    in_specs = [pl.BlockSpec((tm, d_ssm), row), pl.BlockSpec((tm, d_att), row), pl.BlockSpec((tm, d), row),
                _const_spec((d_ssm, d_ssm)), _const_spec((1, d_ssm)), _const_spec((1, d_ssm)),
                _const_spec((1, d_att)), _const_spec((d_ssm, d)), _const_spec((d_att, d)),
                pl.BlockSpec((1, 1, d), bat), _const_spec((1, d)),
                pl.BlockSpec((1, 1, d), bat), pl.BlockSpec((1, 1, d), bat)]
    return pl.pallas_call(
        _out_kernel,
        grid=(t // tm,),
        in_specs=in_specs,
        out_specs=[pl.BlockSpec((tm, d), row), pl.BlockSpec((tm, d), row)],
        out_shape=[jax.ShapeDtypeStruct((t, d), F32), jax.ShapeDtypeStruct((t, d), BF16)],
        compiler_params=_params(("arbitrary",)),
        name="out_proj",
    )(y_ssm, y_att, x2, w_glu.astype(BF16), b_glu.reshape(1, d_ssm), g_ssm.reshape(1, d_ssm),
      g_nsa.reshape(1, d_att), w_out[:d_ssm], w_out[d_ssm:], ga1.reshape(bsz, 1, d), g_ffn.reshape(1, d),
      sh2.reshape(bsz, 1, d), sc2.reshape(bsz, 1, d))


FFN_HALO = 16


def _ffn_kernel(h_ref, halo_ref, x1_ref, wv_ref, wg_ref, cwv_ref, cwg_ref, cbv_ref, cbg_ref, wd_ref,
                ga2_ref, gfin_ref, o_ref, *, nt):
    i = pl.program_id(0)
    j = pl.program_id(1)
    tm = h_ref.shape[0]
    halo = halo_ref[...]
    halo = jnp.where(i % nt == 0, jnp.zeros_like(halo), halo)
    hx = jnp.concatenate([halo, h_ref[...]], axis=0)

    def conv(w_ref, cw_ref, cb_ref):
        up = _dot(hx, w_ref[...])
        cw = cw_ref[...]
        c = (pltpu.roll(up, 2, 0) * cw[0:1] + pltpu.roll(up, 1, 0) * cw[1:2] + up * cw[2:3])
        return c[FFN_HALO:] + cb_ref[...]

    val = conv(wv_ref, cwv_ref, cbv_ref)
    gate = conv(wg_ref, cwg_ref, cbg_ref)
    act = (gate * jax.nn.sigmoid(gate) * val).astype(BF16)
    part = _dot(act, wd_ref[...])

    @pl.when(j == 0)
    def _():
        o_ref[...] = part

    @pl.when(j > 0)
    def _():
        o_ref[...] += part

    @pl.when(j == pl.num_programs(1) - 1)
    def _():
        x2 = x1_ref[...] + ga2_ref[0] * o_ref[...]
        o_ref[...] = _rms(x2, gfin_ref[...])


def _conv_ffn(h2, x1, w_up, conv_w, conv_b, w_down, ga2, g_final, bsz, seq, tm, tn):
    t, d = x1.shape
    dff = w_down.shape[0]
    nt = seq // tm
    nj = dff // tn
    w_up = w_up.astype(BF16)
    w_down = w_down.astype(BF16)
    cb = conv_b.reshape(1, 2 * dff)
    hb = tm // FFN_HALO
    in_specs = [pl.BlockSpec((tm, d), lambda i, j: (i, 0)),
                pl.BlockSpec((FFN_HALO, d), lambda i, j: (jnp.maximum(i * hb - 1, 0), 0)),
                pl.BlockSpec((tm, d), lambda i, j: (i, 0)),
                pl.BlockSpec((d, tn), lambda i, j: (0, j)),
                pl.BlockSpec((d, tn), lambda i, j: (0, nj + j)),
                pl.BlockSpec((CONV_WIDTH, tn), lambda i, j: (0, j)),
                pl.BlockSpec((CONV_WIDTH, tn), lambda i, j: (0, nj + j)),
                pl.BlockSpec((1, tn), lambda i, j: (0, j)),
                pl.BlockSpec((1, tn), lambda i, j: (0, nj + j)),
                pl.BlockSpec((tn, d), lambda i, j: (j, 0)),
                pl.BlockSpec((1, 1, d), lambda i, j: (i // nt, 0, 0)),
                pl.BlockSpec((1, d), lambda i, j: (0, 0))]
    return pl.pallas_call(
        functools.partial(_ffn_kernel, nt=nt),
        grid=(t // tm, nj),
        in_specs=in_specs,
        out_specs=pl.BlockSpec((tm, d), lambda i, j: (i, 0)),
        out_shape=jax.ShapeDtypeStruct((t, d), F32),
        compiler_params=_params(("arbitrary", "arbitrary")),
        name="conv_ffn",
    )(h2, h2, x1, w_up, w_up, conv_w, conv_w, cb, cb, w_down, ga2.reshape(bsz, 1, d), g_final.reshape(1, d))


def _block(x, mod, g_mix_norm, w_in, s5_params, cmp_k, cmp_v, g_ssm_out, g_nsa_out, w_glu, b_glu, w_out,
           g_ffn_norm, w_up, conv_w, conv_b, w_down, g_final):
    bsz, seq, d = x.shape
    t = bsz * seq
    d_ssm = d // 2
    x2 = x.reshape(t, d)
    sh1, sc1, ga1, sh2, sc2, ga2 = jnp.split(mod, 6, axis=-1)
    tm = min(512, seq)

    u, q_hm, kc_tok, vc_tok, kst, vs_hm, kwt, vw_hm, gates = _input_proj(
        x2, g_mix_norm, sh1, sc1, w_in, _rope_tables(seq), bsz, seq, tm)

    ng = d_ssm // SSM_GROUP
    u_g = u.reshape(t // S5_CHUNK, S5_CHUNK, ng, SSM_GROUP).transpose(2, 0, 1, 3).reshape(
        ng, t // S5_CHUNK, S5_CHUNK * SSM_GROUP)
    y_g = _s5_scan(u_g, _s5_prep(*s5_params), bsz)
    y_ssm = y_g.reshape(ng, t // S5_CHUNK, S5_CHUNK, SSM_GROUP).transpose(1, 2, 0, 3).reshape(t, d_ssm)

    kct = _compress(kc_tok, *cmp_k, bsz, seq, True)
    vc_hm = _compress(vc_tok, *cmp_v, bsz, seq, False)
    y_att = _nsa_attention(q_hm, kct, vc_hm, kst, vs_hm, kwt, vw_hm, gates, bsz, seq)

    x1, h2 = _out_proj(y_ssm, y_att, x2, w_glu, b_glu, g_ssm_out, g_nsa_out, w_out, ga1, g_ffn_norm,
                       sh2, sc2, bsz, seq, tm)
    out = _conv_ffn(h2, x1, w_up, conv_w, conv_b, w_down, ga2, g_final, bsz, seq, tm, 512)
    return out.reshape(bsz, seq, d)


def kernel(x, c, w_ada, b_ada, g_mix_norm, w_in, lam_re, lam_im, log_dt, b_re, b_im, c_re, c_im, d_skip,
           w_glu, b_glu, pe_k, w1_k, w2_k, pe_v, w1_v, w2_v, g_ssm_out, g_nsa_out, w_out, g_ffn_norm,
           w_up, conv_w, conv_b, w_down, g_final):
    l = 0
    mod = _adaln_mod(c, w_ada[l], b_ada[l])
    return _block(x, mod, g_mix_norm[l], w_in[l],
                  (lam_re[l], lam_im[l], log_dt[l], b_re[l], b_im[l], c_re[l], c_im[l], d_skip[l]),
                  (pe_k[l], w1_k[l], w2_k[l]), (pe_v[l], w1_v[l], w2_v[l]),
                  g_ssm_out[l], g_nsa_out[l], w_glu[l], b_glu[l], w_out[l], g_ffn_norm[l],
                  w_up[l], conv_w[l], conv_b[l], w_down[l], g_final)
```

```python
import functools
import math

import jax
import jax.numpy as jnp
from jax import lax
from jax.experimental import pallas as pl
from jax.experimental.pallas import tpu as pltpu

F32 = jnp.float32
BF16 = jnp.bfloat16
HIGHEST = lax.Precision.HIGHEST

SSM_GROUP = 16
SSM_STATE = 64
N_HEADS = 16
N_KV_HEADS = 4
HEAD_DIM = 64
GQA = N_HEADS // N_KV_HEADS
KV_DIM = N_KV_HEADS * HEAD_DIM
ROT_DIM = HEAD_DIM // 4
ROPE_THETA = 500000.0
CMP_BLOCK = 32
CMP_STRIDE = 16
CMP_HIDDEN = 2 * HEAD_DIM
SEL_BLOCK = 64
N_SELECT = 16
WINDOW = 512
CONV_WIDTH = 3
NORM_EPS = 1e-6
NEG_BIG = -1e30

LANES = 128
S5_CHUNK = 16
S5_GROUPS_PER_STEP = 8
ATT_TILE = 256
VMEM_LIMIT = 56 * 2**20


def _params(sem, vmem=VMEM_LIMIT):
    return pltpu.CompilerParams(dimension_semantics=sem, vmem_limit_bytes=vmem)


def _const_spec(shape):
    n = len(shape)
    return pl.BlockSpec(shape, lambda *_: (0,) * n)


def _rms(x, g):
    return x * lax.rsqrt(jnp.mean(x * x, axis=-1, keepdims=True) + NORM_EPS) * g


def _dot(a, b):
    return jnp.dot(a, b, preferred_element_type=F32)


def _dot_nt(a, b, precision=None):
    return lax.dot_general(a, b, (((1,), (1,)), ((), ())), precision=precision,
                           preferred_element_type=F32)


def _mod_kernel(c_ref, w_ref, b_ref, o_ref):
    c = c_ref[...]
    sc = c * jax.nn.sigmoid(c)
    o_ref[...] = jnp.dot(sc, w_ref[...], preferred_element_type=F32, precision=HIGHEST) + b_ref[...]


def _adaln_mod(c, w_ada, b_ada):
    bsz, d = c.shape
    n = w_ada.shape[1]
    tn = n // 8
    cp = jnp.zeros((8, d), F32).at[:bsz].set(c)
    out = pl.pallas_call(
        _mod_kernel,
        grid=(n // tn,),
        in_specs=[pl.BlockSpec((8, d), lambda j: (0, 0)),
                  pl.BlockSpec((d, tn), lambda j: (0, j)),
                  pl.BlockSpec((1, tn), lambda j: (0, j))],
        out_specs=pl.BlockSpec((8, tn), lambda j: (0, j)),
        out_shape=jax.ShapeDtypeStruct((8, n), F32),
        compiler_params=_params(("arbitrary",)),
        name="adaln_mod",
    )(cp, w_ada, b_ada.reshape(1, n))
    return out[:bsz]


def _rope_kernel(c_ref, sm_ref, sp_ref, *, tr):
    i = pl.program_id(0)
    pos = (i * tr + lax.broadcasted_iota(jnp.int32, (tr, LANES), 0)).astype(F32)
    lane = lax.broadcasted_iota(jnp.int32, (tr, LANES), 1)
    d = lane & (HEAD_DIM - 1)
    half = ROT_DIM // 2
    fi = d & (half - 1)
    inv = jnp.zeros((tr, LANES), F32)
    for k in range(half):
        inv = jnp.where(fi == k, ROPE_THETA ** (-k / half), inv)
    ang = pos * inv
    cs = jnp.cos(ang)
    sn = jnp.sin(ang)
    c_ref[...] = jnp.where(d < ROT_DIM, cs, 1.0)
    sm_ref[...] = jnp.where(d < half, -sn, 0.0)
    sp_ref[...] = jnp.where((d >= half) & (d < ROT_DIM), sn, 0.0)


def _rope_tables(seq):
    tr = min(seq, 1024)
    sds = jax.ShapeDtypeStruct((seq, LANES), F32)
    spec = pl.BlockSpec((tr, LANES), lambda i: (i, 0))
    return pl.pallas_call(
        functools.partial(_rope_kernel, tr=tr),
        grid=(seq // tr,),
        in_specs=[],
        out_specs=[spec, spec, spec],
        out_shape=[sds, sds, sds],
        compiler_params=_params(("arbitrary",)),
        name="rope_tables",
    )()


def _proj_kernel(x_ref, g_ref, sh_ref, sc_ref, wu_ref, wq_ref, wkv_ref, wg_ref, rc_ref, rm_ref, rp_ref,
                 u_ref, q_ref, kc_ref, vc_ref, kst_ref, vs_ref, kwt_ref, vw_ref, gt_ref):
    x = x_ref[...]
    h = _rms(x, g_ref[...]) * (1.0 + sc_ref[0]) + sh_ref[0]
    hb = h.astype(BF16)
    rc, rm, rp = rc_ref[...], rm_ref[...], rp_ref[...]

    def rope(s):
        return s * rc + pltpu.roll(s, LANES - ROT_DIM // 2, 1) * rm + pltpu.roll(s, ROT_DIM // 2, 1) * rp

    def roped(a):
        return jnp.concatenate([rope(a[:, :LANES]), rope(a[:, LANES:])], axis=1)

    u_ref[...] = _dot(hb, wu_ref[...]).astype(BF16)

    q = _dot(hb, wq_ref[...])
    scale = HEAD_DIM ** -0.5
    for j in range(N_HEADS // 2):
        s = rope(q[:, j * LANES:(j + 1) * LANES]) * scale
        q_ref[0, 2 * j] = s[:, :HEAD_DIM].astype(BF16)
        q_ref[0, 2 * j + 1] = s[:, HEAD_DIM:].astype(BF16)

    kv = _dot(hb, wkv_ref[...])

    def seg(i):
        return kv[:, i * KV_DIM:(i + 1) * KV_DIM]

    kc_ref[...] = roped(seg(0)).astype(BF16)
    vc_ref[...] = seg(1).astype(BF16)
    kst_ref[0] = roped(seg(2)).T.astype(BF16)
    vs = seg(3)
    kwt_ref[0] = roped(seg(4)).T.astype(BF16)
    vw = seg(5)
    ones_col = jnp.where(lax.broadcasted_iota(jnp.int32, (x.shape[0], LANES - HEAD_DIM), 1) == 0, 1.0, 0.0)
    for hh in range(N_KV_HEADS):
        sl = slice(hh * HEAD_DIM, (hh + 1) * HEAD_DIM)
        vs_ref[0, hh] = jnp.concatenate([vs[:, sl], ones_col], axis=1).astype(BF16)
        vw_ref[0, hh] = jnp.concatenate([vw[:, sl], ones_col], axis=1).astype(BF16)

    sg = jax.nn.sigmoid(_dot(hb, wg_ref[...]))
    gt_ref[0] = sg
    for hh in range(1, N_KV_HEADS):
        gt_ref[hh] = pltpu.roll(sg, LANES - 3 * GQA * hh, 1)


def _input_proj(x2, g, sh, sc, w_in, rope_tabs, bsz, seq, tm):
    t, d = x2.shape
    d_ssm = d // 2
    d_att = d - d_ssm
    nt = seq // tm
    w_in = w_in.astype(BF16)
    o = 0
    wu = w_in[:, o:o + d_ssm]; o += d_ssm
    wq = w_in[:, o:o + d_att]; o += d_att
    wkv = w_in[:, o:o + 6 * KV_DIM]; o += 6 * KV_DIM
    wg = jnp.zeros((d, LANES), BF16).at[:, :3 * N_HEADS].set(w_in[:, o:])
    row = lambda i: (i, 0)
    bat = lambda i: (i // nt, 0, 0)
    tab = lambda i: (i % nt, 0)
    hm = lambda i: (i // nt, 0, i % nt, 0)
    tr = lambda i: (i // nt, 0, i % nt)
    in_specs = [pl.BlockSpec((tm, d), row), _const_spec((1, d)),
                pl.BlockSpec((1, 1, d), bat), pl.BlockSpec((1, 1, d), bat),
                _const_spec(wu.shape), _const_spec(wq.shape), _const_spec(wkv.shape), _const_spec(wg.shape),
                pl.BlockSpec((tm, LANES), tab), pl.BlockSpec((tm, LANES), tab), pl.BlockSpec((tm, LANES), tab)]
    out_shape = [jax.ShapeDtypeStruct((t, d_ssm), BF16),
                 jax.ShapeDtypeStruct((bsz, N_HEADS, seq, HEAD_DIM), BF16),
                 jax.ShapeDtypeStruct((t, KV_DIM), BF16),
                 jax.ShapeDtypeStruct((t, KV_DIM), BF16),
                 jax.ShapeDtypeStruct((bsz, KV_DIM, seq), BF16),
                 jax.ShapeDtypeStruct((bsz, N_KV_HEADS, seq, LANES), BF16),
                 jax.ShapeDtypeStruct((bsz, KV_DIM, seq), BF16),
                 jax.ShapeDtypeStruct((bsz, N_KV_HEADS, seq, LANES), BF16),
                 jax.ShapeDtypeStruct((N_KV_HEADS, t, LANES), F32)]
    out_specs = [pl.BlockSpec((tm, d_ssm), row),
                 pl.BlockSpec((1, N_HEADS, tm, HEAD_DIM), hm),
                 pl.BlockSpec((tm, KV_DIM), row),
                 pl.BlockSpec((tm, KV_DIM), row),
                 pl.BlockSpec((1, KV_DIM, tm), tr),
                 pl.BlockSpec((1, N_KV_HEADS, tm, LANES), hm),
                 pl.BlockSpec((1, KV_DIM, tm), tr),
                 pl.BlockSpec((1, N_KV_HEADS, tm, LANES), hm),
                 pl.BlockSpec((N_KV_HEADS, tm, LANES), lambda i: (0, i, 0))]
    return pl.pallas_call(
        _proj_kernel,
        grid=(t // tm,),
        in_specs=in_specs, out_specs=out_specs, out_shape=out_shape,
        compiler_params=_params(("arbitrary",)),
        name="input_proj",
    )(x2, g.reshape(1, d), sh.reshape(bsz, 1, d), sc.reshape(bsz, 1, d), wu, wq, wkv, wg, *rope_tabs)


def _s5prep_kernel(lr_ref, li_ref, ldt_ref, btr_ref, bti_ref, cr_ref, ci_ref, dsk_ref,
                   m_ref, bzr_ref, bzi_ref, cyr_ref, cyi_ref, ar_ref, ai_ref):
    L, H, P = S5_CHUNK, SSM_GROUP, SSM_STATE
    lr = lr_ref[0]
    li = li_ref[0]
    dt = jnp.exp(ldt_ref[0])
    mag = jnp.exp(lr * dt)
    ab_re = mag * jnp.cos(li * dt)
    ab_im = mag * jnp.sin(li * dt)
    den = lr * lr + li * li
    num_re = ab_re - 1.0
    f_re = (num_re * lr + ab_im * li) / den
    f_im = (ab_im * lr - num_re * li) / den
    btr = btr_ref[0]
    bti = bti_ref[0]
    bb_re = f_re * btr - f_im * bti
    bb_im = f_re * bti + f_im * btr
    c_re = cr_ref[0]
    c_im = ci_ref[0]

    r = lax.broadcasted_iota(jnp.int32, (4 * L, 1), 0)
    k = r & (L - 1)
    blk = r >> 4
    e = jnp.where(blk == 0, -k, jnp.where(blk == 1, k, jnp.where(blk == 2, L - 1 - k, k + 1))).astype(F32)
    pm = jnp.exp(lr * dt * e)
    ph = li * dt * e
    pr = pm * jnp.cos(ph)
    pi = pm * jnp.sin(ph)

    def rows(i, sm_r, sm_i):
        a_r = jnp.broadcast_to(pr[i * L:(i + 1) * L][:, None, :], (L, H, P)).reshape(L * H, P)
        a_i = jnp.broadcast_to(pi[i * L:(i + 1) * L][:, None, :], (L, H, P)).reshape(L * H, P)
        s_r = jnp.broadcast_to(sm_r[None], (L, H, P)).reshape(L * H, P)
        s_i = jnp.broadcast_to(sm_i[None], (L, H, P)).reshape(L * H, P)
        return a_r * s_r - a_i * s_i, a_r * s_i + a_i * s_r

    x_re, x_im = rows(0, bb_re, bb_im)
    y_re, y_im = rows(1, c_re, c_im)
    z_re, z_im = rows(2, bb_re, bb_im)
    w_re, w_im = rows(3, c_re, c_im)
    mm = _dot_nt(x_re, y_re, HIGHEST) - _dot_nt(x_im, y_im, HIGHEST)
    ri = lax.broadcasted_iota(jnp.int32, (L * H, L * H), 0)
    ci = lax.broadcasted_iota(jnp.int32, (L * H, L * H), 1)
    mm = jnp.where((ci >> 4) >= (ri >> 4), mm, 0.0) + jnp.where(ri == ci, dsk_ref[0], 0.0)
    m_ref[0] = mm.astype(BF16)
    bzr_ref[0] = z_re.astype(BF16)
    bzi_ref[0] = z_im.astype(BF16)
    cyr_ref[0] = w_re.astype(BF16)
    cyi_ref[0] = (-w_im).astype(BF16)
    m16 = jnp.exp(lr * dt * L)
    ar_ref[0] = m16 * jnp.cos(li * dt * L)
    ai_ref[0] = m16 * jnp.sin(li * dt * L)


def _s5_prep(lam_re, lam_im, log_dt, b_re, b_im, c_re, c_im, d_skip):
    g, p, h = b_re.shape
    lh = S5_CHUNK * h
    g3 = lambda i: (i, 0, 0)
    ins = [lam_re.reshape(g, 1, p), lam_im.reshape(g, 1, p),
           jnp.broadcast_to(log_dt.reshape(g, 1, 1), (g, 1, p)),
           b_re.transpose(0, 2, 1), b_im.transpose(0, 2, 1), c_re, c_im,
           jnp.tile(d_skip, (1, S5_CHUNK)).reshape(g, 1, lh)]
    in_specs = [pl.BlockSpec((1,) + a.shape[1:], g3) for a in ins]
    out_shape = [jax.ShapeDtypeStruct((g, lh, lh), BF16)] + \
                [jax.ShapeDtypeStruct((g, lh, p), BF16)] * 4 + \
                [jax.ShapeDtypeStruct((g, 1, p), F32)] * 2
    out_specs = [pl.BlockSpec((1,) + s.shape[1:], g3) for s in out_shape]
    return pl.pallas_call(
        _s5prep_kernel, grid=(g,), in_specs=in_specs, out_specs=out_specs, out_shape=out_shape,
        compiler_params=_params(("arbitrary",)), name="s5_prep",
    )(*ins)


def _s5_kernel(u_ref, m_ref, bzr_ref, bzi_ref, cyr_ref, cyi_ref, ar_ref, ai_ref, y_ref,
               yi_sc, zr_sc, zi_sc, xr_sc, xi_sc, *, gb, nch, bsz):
    P = SSM_STATE
    for g in range(gb):
        ug = u_ref[g]
        yi_sc[g] = _dot(ug, m_ref[g])
        zr_sc[:, g * P:(g + 1) * P] = _dot(ug, bzr_ref[g])
        zi_sc[:, g * P:(g + 1) * P] = _dot(ug, bzi_ref[g])
    a_r = ar_ref[0]
    a_i = ai_ref[0]

    def body(c, carry):
        new = []
        for b in range(bsz):
            x_r, x_i = carry[2 * b], carry[2 * b + 1]
            row = b * nch + c
            xr_sc[pl.ds(row, 1), :] = x_r
            xi_sc[pl.ds(row, 1), :] = x_i
            z_r = zr_sc[pl.ds(row, 1), :]
            z_i = zi_sc[pl.ds(row, 1), :]
            new.append(a_r * x_r - a_i * x_i + z_r)
            new.append(a_r * x_i + a_i * x_r + z_i)
        return tuple(new)

    zero = jnp.zeros((1, gb * P), F32)
    lax.fori_loop(0, nch, body, (zero,) * (2 * bsz))
    for g in range(gb):
        x_r = xr_sc[:, g * P:(g + 1) * P].astype(BF16)
        x_i = xi_sc[:, g * P:(g + 1) * P].astype(BF16)
        y = yi_sc[g] + _dot_nt(x_r, cyr_ref[g]) + _dot_nt(x_i, cyi_ref[g])
        y_ref[g] = y.astype(BF16)


def _s5_scan(u_g, ops, bsz):
    m, bzr, bzi, cyr, cyi, a_r, a_i = ops
    g, rows, lh = u_g.shape
    p = SSM_STATE
    gb = S5_GROUPS_PER_STEP
    nch = rows // bsz
    a_r = a_r.reshape(g // gb, 1, gb * p)
    a_i = a_i.reshape(g // gb, 1, gb * p)
    g3 = lambda i: (i, 0, 0)
    in_specs = [pl.BlockSpec((gb, rows, lh), g3), pl.BlockSpec((gb, lh, lh), g3)] + \
               [pl.BlockSpec((gb, lh, p), g3)] * 4 + [pl.BlockSpec((1, 1, gb * p), g3)] * 2
    return pl.pallas_call(
        functools.partial(_s5_kernel, gb=gb, nch=nch, bsz=bsz),
        grid=(g // gb,),
        in_specs=in_specs,
        out_specs=pl.BlockSpec((gb, rows, lh), g3),
        out_shape=jax.ShapeDtypeStruct((g, rows, lh), BF16),
        scratch_shapes=[pltpu.VMEM((gb, rows, lh), F32)] + [pltpu.VMEM((rows, gb * p), F32)] * 4,
        compiler_params=_params(("arbitrary",)),
        name="s5_scan",
    )(u_g, m, bzr, bzi, cyr, cyi, a_r, a_i)


def _cmp_kernel(tok_ref, w1x_ref, pe_ref, w1_ref, w2x_ref, o_ref, *, transpose_out):
    nh = N_KV_HEADS * CMP_HIDDEN
    g = _dot(tok_ref[...], w1x_ref[...])
    rows = g.shape[0]
    bias = jnp.dot(jnp.broadcast_to(pe_ref[...], (8, pe_ref.shape[1])), w1_ref[...],
                   preferred_element_type=F32, precision=HIGHEST)[0:1]
    bias = jnp.concatenate([bias] * N_KV_HEADS, axis=1)
    hid = g[:, :nh] + pltpu.roll(g[:, nh:], rows - 1, 0) + bias
    out = _dot(jax.nn.gelu(hid).astype(BF16), w2x_ref[...])
    if transpose_out:
        o_ref[0] = out.T.astype(BF16)
    else:
        for hh in range(N_KV_HEADS):
            o_ref[0, hh] = out[:, hh * HEAD_DIM:(hh + 1) * HEAD_DIM].astype(BF16)


def _compress(tok, pe, w1, w2, bsz, seq, transpose_out):
    ncp = seq // CMP_STRIDE
    half = CMP_STRIDE * HEAD_DIM
    eye = jnp.eye(N_KV_HEADS, dtype=F32)
    w1r = w1.reshape(2, CMP_STRIDE, HEAD_DIM, CMP_HIDDEN)
    w1x = jnp.einsum('zkdj,hg->khdzgj', w1r, eye).reshape(CMP_STRIDE * KV_DIM, 2 * N_KV_HEADS * CMP_HIDDEN)
    w2x = jnp.einsum('jd,hg->hjgd', w2, eye).reshape(N_KV_HEADS * CMP_HIDDEN, KV_DIM)
    tok2 = tok.reshape(bsz * ncp, CMP_STRIDE * KV_DIM)
    if transpose_out:
        out_shape = jax.ShapeDtypeStruct((bsz, KV_DIM, ncp), BF16)
        out_spec = pl.BlockSpec((1, KV_DIM, ncp), lambda b: (b, 0, 0))
    else:
        out_shape = jax.ShapeDtypeStruct((bsz, N_KV_HEADS, ncp, HEAD_DIM), BF16)
        out_spec = pl.BlockSpec((1, N_KV_HEADS, ncp, HEAD_DIM), lambda b: (b, 0, 0, 0))
    return pl.pallas_call(
        functools.partial(_cmp_kernel, transpose_out=transpose_out),
        grid=(bsz,),
        in_specs=[pl.BlockSpec((ncp, CMP_STRIDE * KV_DIM), lambda b: (b, 0)),
                  _const_spec(w1x.shape), _const_spec((1, 2 * half)), _const_spec(w1.shape),
                  _const_spec(w2x.shape)],
        out_specs=out_spec, out_shape=out_shape,
        compiler_params=_params(("arbitrary",)),
        name="compress_k" if transpose_out else "compress_v",
    )(tok2, w1x.astype(BF16), pe.reshape(1, 2 * half), w1, w2x.astype(BF16))


def _nsa_kernel(q_ref, kct_ref, vc_ref, kst_ref, vs_ref, kwt_ref, vw_ref, gt_ref, amat_ref, o_ref,
                rhs_sc, lhs_sc, m_sc, st_sc, *, tq, seq):
    nblk = seq // SEL_BLOCK
    ncp = seq // CMP_STRIDE
    rows = GQA * tq
    nslab = tq // LANES
    qi = pl.program_id(2)
    t0 = qi * tq

    @pl.when(qi == 0)
    def _():
        rhs_sc[0:HEAD_DIM, :] = kst_ref[0]
        rhs_sc[HEAD_DIM:LANES, :] = jnp.zeros((LANES - HEAD_DIM, seq), BF16)
        cw = min(seq, 1024)

        def fill(c, _):
            col = c * cw + lax.broadcasted_iota(jnp.int32, (nblk, cw), 1)
            blk = lax.broadcasted_iota(jnp.int32, (nblk, cw), 0)
            onehot = jnp.where((col >> 6) == blk, 1.0, 0.0).astype(BF16)
            rhs_sc[LANES:LANES + nblk, pl.ds(pl.multiple_of(c * cw, cw), cw)] = onehot
            return 0

        lax.fori_loop(0, seq // cw, fill, 0)

    t_q = t0 + lax.broadcasted_iota(jnp.int32, (tq, 1), 0)

    n_col = lax.broadcasted_iota(jnp.int32, (1, ncp), 1)
    valid = (n_col * CMP_STRIDE + (CMP_BLOCK - 1)) <= t_q
    kct = kct_ref[0]
    vc = vc_ref[0, 0]
    o_cmp = []
    psum = None
    for g in range(GQA):
        sm = jnp.where(valid, _dot(q_ref[0, g], kct), NEG_BIG)
        e = jnp.exp(sm - jnp.max(sm, axis=-1, keepdims=True))
        p = jnp.where(valid, e * (1.0 / jnp.sum(e, axis=-1, keepdims=True)), 0.0)
        o_cmp.append(_dot(p.astype(BF16), vc))
        psum = p if g == 0 else psum + p

    amat = amat_ref[...]
    p_hi = psum.astype(BF16)
    r1 = psum - p_hi.astype(F32)
    p_mid = r1.astype(BF16)
    p_lo = (r1 - p_mid.astype(F32)).astype(BF16)
    imp = _dot(p_hi, amat) + _dot(p_mid, amat) + _dot(p_lo, amat)
    blk = lax.broadcasted_iota(jnp.int32, (1, nblk), 1)
    blk_f = blk.astype(F32)
    cur = t_q >> 6
    forced = (blk == 0) | (blk == cur) | (blk == cur - 1)
    impm = jnp.where(forced, -NEG_BIG, jnp.where(blk * SEL_BLOCK <= t_q, imp, NEG_BIG))
    selneg = jnp.full((tq, nblk), NEG_BIG, F32)
    for _ in range(min(N_SELECT, nblk)):
        best = jnp.max(impm, axis=-1, keepdims=True)
        idx = jnp.min(jnp.where(impm == best, blk_f, float(nblk)), axis=-1, keepdims=True)
        pick = blk_f == idx
        selneg = jnp.where(pick, 0.0, selneg)
        impm = jnp.where(pick, -jnp.inf, impm)
    selneg = selneg.astype(BF16)

    for g in range(GQA):
        sl = slice(g * tq, (g + 1) * tq)
        lhs_sc[sl, 0:HEAD_DIM] = q_ref[0, g]
        lhs_sc[sl, HEAD_DIM:LANES] = jnp.zeros((tq, LANES - HEAD_DIM), BF16)
        lhs_sc[sl, LANES:LANES + nblk] = selneg

    def reset():
        m_sc[...] = jnp.full((rows, LANES), NEG_BIG, F32)
        st_sc[...] = jnp.zeros((rows, LANES), F32)

    def online(g, s, v_aug):
        sl = slice(g * tq, (g + 1) * tq)
        m_old = m_sc[sl, :]
        mx = s[:, 0:LANES]
        for j in range(1, nslab):
            mx = jnp.maximum(mx, s[:, j * LANES:(j + 1) * LANES])
        m_new = jnp.maximum(m_old, jnp.max(mx, axis=-1, keepdims=True))
        alpha = jnp.exp(m_old - m_new)
        p = jnp.concatenate([jnp.exp(s[:, j * LANES:(j + 1) * LANES] - m_new) for j in range(nslab)], axis=1)
        st_sc[sl, :] = alpha * st_sc[sl, :] + _dot(p.astype(BF16), v_aug)
        m_sc[sl, :] = m_new

    def result(g):
        st = st_sc[g * tq:(g + 1) * tq, :]
        return st[:, 0:HEAD_DIM] * (1.0 / st[:, HEAD_DIM:HEAD_DIM + 1])

    def kpos(kj):
        return kj * tq + lax.broadcasted_iota(jnp.int32, (1, tq), 1)

    reset()

    def sel_tile(kj, mask):
        off = pl.multiple_of(kj * tq, tq)
        rhs = rhs_sc[:, pl.ds(off, tq)]
        v_aug = vs_ref[0, 0, pl.ds(off, tq), :]
        for g in range(GQA):
            s = _dot(lhs_sc[g * tq:(g + 1) * tq, :], rhs)
            if mask is not None:
                s = jnp.where(mask, s, NEG_BIG)
            online(g, s, v_aug)

    def sel_step(kj, _):
        sel_tile(kj, None)
        return 0

    lax.fori_loop(0, qi, sel_step, 0)
    sel_tile(qi, kpos(qi) <= t_q)
    o_sel = [result(g) for g in range(GQA)]

    reset()

    def win_step(kj, _):
        off = pl.multiple_of(kj * tq, tq)
        kt = kwt_ref[0, :, pl.ds(off, tq)]
        v_aug = vw_ref[0, 0, pl.ds(off, tq), :]
        kp = kpos(kj)
        mask = (kp <= t_q) & (kp > t_q - WINDOW)
        for g in range(GQA):
            online(g, jnp.where(mask, _dot(q_ref[0, g], kt), NEG_BIG), v_aug)
        return 0

    lax.fori_loop(jnp.maximum(qi - WINDOW // tq, 0), qi + 1, win_step, 0)

    gt = gt_ref[0]
    for g in range(GQA):
        o = (gt[:, 3 * g:3 * g + 1] * o_cmp[g] + gt[:, 3 * g + 1:3 * g + 2] * o_sel[g]
             + gt[:, 3 * g + 2:3 * g + 3] * result(g))
        o_ref[:, g * HEAD_DIM:(g + 1) * HEAD_DIM] = o


def _importance_matrix(seq):
    ncp = seq // CMP_STRIDE
    nblk = seq // SEL_BLOCK
    ratio = SEL_BLOCK // CMP_STRIDE
    n = jnp.arange(ncp)[:, None]
    j = jnp.arange(nblk)[None, :]
    a = sum(((n == ratio * j + r).astype(F32) + (n == ratio * j + r + 1).astype(F32)) for r in range(ratio))
    return jnp.where(n < ncp - 1, a, 0.0).astype(BF16)


def _nsa_attention(q_hm, kct, vc_hm, kst, vs_hm, kwt, vw_hm, gates, bsz, seq):
    tq = min(ATT_TILE, seq)
    nq = seq // tq
    nblk = seq // SEL_BLOCK
    ncp = seq // CMP_STRIDE
    rows = GQA * tq
    amat = _importance_matrix(seq)
    head = lambda b, h, i: (b, h, 0, 0)
    headt = lambda b, h, i: (b, h, 0)
    in_specs = [pl.BlockSpec((1, GQA, tq, HEAD_DIM), lambda b, h, i: (b, h, i, 0)),
                pl.BlockSpec((1, HEAD_DIM, ncp), headt),
                pl.BlockSpec((1, 1, ncp, HEAD_DIM), head),
                pl.BlockSpec((1, HEAD_DIM, seq), headt),
                pl.BlockSpec((1, 1, seq, LANES), head),
                pl.BlockSpec((1, HEAD_DIM, seq), headt),
                pl.BlockSpec((1, 1, seq, LANES), head),
                pl.BlockSpec((1, tq, LANES), lambda b, h, i: (h, b * nq + i, 0)),
                _const_spec(amat.shape)]
    return pl.pallas_call(
        functools.partial(_nsa_kernel, tq=tq, seq=seq),
        grid=(bsz, N_KV_HEADS, nq),
        in_specs=in_specs,
        out_specs=pl.BlockSpec((tq, GQA * HEAD_DIM), lambda b, h, i: (b * nq + i, h)),
        out_shape=jax.ShapeDtypeStruct((bsz * seq, N_HEADS * HEAD_DIM), F32),
        scratch_shapes=[pltpu.VMEM((LANES + nblk, seq), BF16),
                        pltpu.VMEM((rows, LANES + nblk), BF16),
                        pltpu.VMEM((rows, LANES), F32), pltpu.VMEM((rows, LANES), F32)],
        compiler_params=_params(("arbitrary", "arbitrary", "arbitrary")),
        name="nsa_attention",
    )(q_hm, kct, vc_hm, kst, vs_hm, kwt, vw_hm, gates, amat)


def _out_kernel(ys_ref, ya_ref, x_ref, wglu_ref, bglu_ref, gs_ref, ga_ref, woa_ref, wob_ref,
                ga1_ref, gf_ref, sh2_ref, sc2_ref, x1_ref, h2_ref):
    z = jax.nn.gelu(ys_ref[...].astype(F32))
    glu = z * jax.nn.sigmoid(_dot(z.astype(BF16), wglu_ref[...]) + bglu_ref[...])
    n1 = _rms(glu, gs_ref[...]).astype(BF16)
    n2 = _rms(ya_ref[...], ga_ref[...]).astype(BF16)
    y = _dot(n1, woa_ref[...]) + _dot(n2, wob_ref[...])
    x1 = x_ref[...] + ga1_ref[0] * y
    x1_ref[...] = x1
    h2_ref[...] = (_rms(x1, gf_ref[...]) * (1.0 + sc2_ref[0]) + sh2_ref[0]).astype(BF16)


def _out_proj(y_ssm, y_att, x2, w_glu, b_glu, g_ssm, g_nsa, w_out, ga1, g_ffn, sh2, sc2, bsz, seq, tm):
    t, d = x2.shape
    d_ssm = y_ssm.shape[1]
    d_att = y_att.shape[1]
    nt = seq // tm
    row = lambda i: (i, 0)
    bat = lambda i: (i // nt, 0, 0)
    w_out = w_out.astype(BF16)
    in_specs = [pl.BlockSpec((tm, d_ssm), row), pl.BlockSpec((tm, d_att), row), pl.BlockSpec((tm, d), row),
                _const_spec((d_ssm, d_ssm)), _const_spec((1, d_ssm)), _const_spec((1, d_ssm)),
                _const_spec((1, d_att)), _const_spec((d_ssm, d)), _const_spec((d_att, d)),
                pl.BlockSpec((1, 1, d), bat), _const_spec((1, d)),
                pl.BlockSpec((1, 1, d), bat), pl.BlockSpec((1, 1, d), bat)]
    return pl.pallas_call(
        _out_kernel,
        grid=(t // tm,),
        in_specs=in_specs,
        out_specs=[pl.BlockSpec((tm, d), row), pl.BlockSpec((tm, d), row)],
        out_shape=[jax.ShapeDtypeStruct((t, d), F32), jax.ShapeDtypeStruct((t, d), BF16)],
        compiler_params=_params(("arbitrary",)),
        name="out_proj",
    )(y_ssm, y_att, x2, w_glu.astype(BF16), b_glu.reshape(1, d_ssm), g_ssm.reshape(1, d_ssm),
      g_nsa.reshape(1, d_att), w_out[:d_ssm], w_out[d_ssm:], ga1.reshape(bsz, 1, d), g_ffn.reshape(1, d),
      sh2.reshape(bsz, 1, d), sc2.reshape(bsz, 1, d))


FFN_HALO = 16


def _ffn_kernel(h_ref, halo_ref, x1_ref, wv_ref, wg_ref, cwv_ref, cwg_ref, cbv_ref, cbg_ref, wd_ref,
                ga2_ref, gfin_ref, o_ref, *, nt):
    i = pl.program_id(0)
    j = pl.program_id(1)
    tm = h_ref.shape[0]
    halo = halo_ref[...]
    halo = jnp.where(i % nt == 0, jnp.zeros_like(halo), halo)
    hx = jnp.concatenate([halo, h_ref[...]], axis=0)

    def conv(w_ref, cw_ref, cb_ref):
        up = _dot(hx, w_ref[...])
        cw = cw_ref[...]
        c = (pltpu.roll(up, 2, 0) * cw[0:1] + pltpu.roll(up, 1, 0) * cw[1:2] + up * cw[2:3])
        return c[FFN_HALO:] + cb_ref[...]

    val = conv(wv_ref, cwv_ref, cbv_ref)
    gate = conv(wg_ref, cwg_ref, cbg_ref)
    act = (gate * jax.nn.sigmoid(gate) * val).astype(BF16)
    part = _dot(act, wd_ref[...])

    @pl.when(j == 0)
    def _():
        o_ref[...] = part

    @pl.when(j > 0)
    def _():
        o_ref[...] += part

    @pl.when(j == pl.num_programs(1) - 1)
    def _():
        x2 = x1_ref[...] + ga2_ref[0] * o_ref[...]
        o_ref[...] = _rms(x2, gfin_ref[...])


def _conv_ffn(h2, x1, w_up, conv_w, conv_b, w_down, ga2, g_final, bsz, seq, tm, tn):
    t, d = x1.shape
    dff = w_down.shape[0]
    nt = seq // tm
    nj = dff // tn
    w_up = w_up.astype(BF16)
    w_down = w_down.astype(BF16)
    cb = conv_b.reshape(1, 2 * dff)
    hb = tm // FFN_HALO
    in_specs = [pl.BlockSpec((tm, d), lambda i, j: (i, 0)),
                pl.BlockSpec((FFN_HALO, d), lambda i, j: (jnp.maximum(i * hb - 1, 0), 0)),
                pl.BlockSpec((tm, d), lambda i, j: (i, 0)),
                pl.BlockSpec((d, tn), lambda i, j: (0, j)),
                pl.BlockSpec((d, tn), lambda i, j: (0, nj + j)),
                pl.BlockSpec((CONV_WIDTH, tn), lambda i, j: (0, j)),
                pl.BlockSpec((CONV_WIDTH, tn), lambda i, j: (0, nj + j)),
                pl.BlockSpec((1, tn), lambda i, j: (0, j)),
                pl.BlockSpec((1, tn), lambda i, j: (0, nj + j)),
                pl.BlockSpec((tn, d), lambda i, j: (j, 0)),
                pl.BlockSpec((1, 1, d), lambda i, j: (i // nt, 0, 0)),
                pl.BlockSpec((1, d), lambda i, j: (0, 0))]
    return pl.pallas_call(
        functools.partial(_ffn_kernel, nt=nt),
        grid=(t // tm, nj),
        in_specs=in_specs,
        out_specs=pl.BlockSpec((tm, d), lambda i, j: (i, 0)),
        out_shape=jax.ShapeDtypeStruct((t, d), F32),
        compiler_params=_params(("arbitrary", "arbitrary")),
        name="conv_ffn",
    )(h2, h2, x1, w_up, w_up, conv_w, conv_w, cb, cb, w_down, ga2.reshape(bsz, 1, d), g_final.reshape(1, d))


def _block(x, mod, g_mix_norm, w_in, s5_params, cmp_k, cmp_v, g_ssm_out, g_nsa_out, w_glu, b_glu, w_out,
           g_ffn_norm, w_up, conv_w, conv_b, w_down, g_final):
    bsz, seq, d = x.shape
    t = bsz * seq
    d_ssm = d // 2
    x2 = x.reshape(t, d)
    sh1, sc1, ga1, sh2, sc2, ga2 = jnp.split(mod, 6, axis=-1)
    tm = min(512, seq)

    u, q_hm, kc_tok, vc_tok, kst, vs_hm, kwt, vw_hm, gates = _input_proj(
        x2, g_mix_norm, sh1, sc1, w_in, _rope_tables(seq), bsz, seq, tm)

    ng = d_ssm // SSM_GROUP
    u_g = u.reshape(t // S5_CHUNK, S5_CHUNK, ng, SSM_GROUP).transpose(2, 0, 1, 3).reshape(
        ng, t // S5_CHUNK, S5_CHUNK * SSM_GROUP)
    y_g = _s5_scan(u_g, _s5_prep(*s5_params), bsz)
    y_ssm = y_g.reshape(ng, t // S5_CHUNK, S5_CHUNK, SSM_GROUP).transpose(1, 2, 0, 3).reshape(t, d_ssm)

    kct = _compress(kc_tok, *cmp_k, bsz, seq, True)
    vc_hm = _compress(vc_tok, *cmp_v, bsz, seq, False)
    y_att = _nsa_attention(q_hm, kct, vc_hm, kst, vs_hm, kwt, vw_hm, gates, bsz, seq)

    x1, h2 = _out_proj(y_ssm, y_att, x2, w_glu, b_glu, g_ssm_out, g_nsa_out, w_out, ga1, g_ffn_norm,
                       sh2, sc2, bsz, seq, tm)
    out = _conv_ffn(h2, x1, w_up, conv_w, conv_b, w_down, ga2, g_final, bsz, seq, tm, 512)
    return out.reshape(bsz, seq, d)


def kernel(x, c, w_ada, b_ada, g_mix_norm, w_in, lam_re, lam_im, log_dt, b_re, b_im, c_re, c_im, d_skip,
           w_glu, b_glu, pe_k, w1_k, w2_k, pe_v, w1_v, w2_v, g_ssm_out, g_nsa_out, w_out, g_ffn_norm,
           w_up, conv_w, conv_b, w_down, g_final):
    l = 0
    mod = _adaln_mod(c, w_ada[l], b_ada[l])
    return _block(x, mod, g_mix_norm[l], w_in[l],
                  (lam_re[l], lam_im[l], log_dt[l], b_re[l], b_im[l], c_re[l], c_im[l], d_skip[l]),
                  (pe_k[l], w1_k[l], w2_k[l]), (pe_v[l], w1_v[l], w2_v[l]),
                  g_ssm_out[l], g_nsa_out[l], w_glu[l], b_glu[l], w_out[l], g_ffn_norm[l],
                  w_up[l], conv_w[l], conv_b[l], w_down[l], g_final)
```

```python
import functools
import math

import jax
import jax.numpy as jnp
from jax import lax
from jax.experimental import pallas as pl
from jax.experimental.pallas import tpu as pltpu

F32 = jnp.float32
BF16 = jnp.bfloat16
HIGHEST = lax.Precision.HIGHEST

SSM_GROUP = 16
SSM_STATE = 64
N_HEADS = 16
N_KV_HEADS = 4
HEAD_DIM = 64
GQA = N_HEADS // N_KV_HEADS
KV_DIM = N_KV_HEADS * HEAD_DIM
ROT_DIM = HEAD_DIM // 4
ROPE_THETA = 500000.0
CMP_BLOCK = 32
CMP_STRIDE = 16
CMP_HIDDEN = 2 * HEAD_DIM
SEL_BLOCK = 64
N_SELECT = 16
WINDOW = 512
CONV_WIDTH = 3
NORM_EPS = 1e-6
NEG_BIG = -1e30

LANES = 128
S5_CHUNK = 16
S5_GROUPS_PER_STEP = 8
ATT_TILE = 256
VMEM_LIMIT = 56 * 2**20


def _params(sem, vmem=VMEM_LIMIT):
    return pltpu.CompilerParams(dimension_semantics=sem, vmem_limit_bytes=vmem)


def _const_spec(shape):
    n = len(shape)
    return pl.BlockSpec(shape, lambda *_: (0,) * n)


def _rms(x, g):
    return x * lax.rsqrt(jnp.mean(x * x, axis=-1, keepdims=True) + NORM_EPS) * g


def _dot(a, b):
    return jnp.dot(a, b, preferred_element_type=F32)


def _dot_nt(a, b, precision=None):
    return lax.dot_general(a, b, (((1,), (1,)), ((), ())), precision=precision,
                           preferred_element_type=F32)


def _mod_kernel(c_ref, w_ref, b_ref, o_ref):
    c = c_ref[...]
    sc = c * jax.nn.sigmoid(c)
    o_ref[...] = jnp.dot(sc, w_ref[...], preferred_element_type=F32, precision=HIGHEST) + b_ref[...]


def _adaln_mod(c, w_ada, b_ada):
    bsz, d = c.shape
    n = w_ada.shape[1]
    tn = n // 8
    cp = jnp.zeros((8, d), F32).at[:bsz].set(c)
    out = pl.pallas_call(
        _mod_kernel,
        grid=(n // tn,),
        in_specs=[pl.BlockSpec((8, d), lambda j: (0, 0)),
                  pl.BlockSpec((d, tn), lambda j: (0, j)),
                  pl.BlockSpec((1, tn), lambda j: (0, j))],
        out_specs=pl.BlockSpec((8, tn), lambda j: (0, j)),
        out_shape=jax.ShapeDtypeStruct((8, n), F32),
        compiler_params=_params(("arbitrary",)),
        name="adaln_mod",
    )(cp, w_ada, b_ada.reshape(1, n))
    return out[:bsz]


def _rope_kernel(c_ref, sm_ref, sp_ref, *, tr):
    i = pl.program_id(0)
    pos = (i * tr + lax.broadcasted_iota(jnp.int32, (tr, LANES), 0)).astype(F32)
    lane = lax.broadcasted_iota(jnp.int32, (tr, LANES), 1)
    d = lane & (HEAD_DIM - 1)
    half = ROT_DIM // 2
    fi = d & (half - 1)
    inv = jnp.zeros((tr, LANES), F32)
    for k in range(half):
        inv = jnp.where(fi == k, ROPE_THETA ** (-k / half), inv)
    ang = pos * inv
    cs = jnp.cos(ang)
    sn = jnp.sin(ang)
    c_ref[...] = jnp.where(d < ROT_DIM, cs, 1.0)
    sm_ref[...] = jnp.where(d < half, -sn, 0.0)
    sp_ref[...] = jnp.where((d >= half) & (d < ROT_DIM), sn, 0.0)


def _rope_tables(seq):
    tr = min(seq, 1024)
    sds = jax.ShapeDtypeStruct((seq, LANES), F32)
    spec = pl.BlockSpec((tr, LANES), lambda i: (i, 0))
    return pl.pallas_call(
        functools.partial(_rope_kernel, tr=tr),
        grid=(seq // tr,),
        in_specs=[],
        out_specs=[spec, spec, spec],
        out_shape=[sds, sds, sds],
        compiler_params=_params(("arbitrary",)),
        name="rope_tables",
    )()


def _proj_kernel(x_ref, g_ref, sh_ref, sc_ref, wu_ref, wq_ref, wkv_ref, wg_ref, rc_ref, rm_ref, rp_ref,
                 u_ref, q_ref, kc_ref, vc_ref, kst_ref, vs_ref, kwt_ref, vw_ref, gt_ref):
    x = x_ref[...]
    h = _rms(x, g_ref[...]) * (1.0 + sc_ref[0]) + sh_ref[0]
    hb = h.astype(BF16)
    rc, rm, rp = rc_ref[...], rm_ref[...], rp_ref[...]

    def rope(s):
        return s * rc + pltpu.roll(s, LANES - ROT_DIM // 2, 1) * rm + pltpu.roll(s, ROT_DIM // 2, 1) * rp

    def roped(a):
        return jnp.concatenate([rope(a[:, :LANES]), rope(a[:, LANES:])], axis=1)

    u_ref[...] = _dot(hb, wu_ref[...]).astype(BF16)

    q = _dot(hb, wq_ref[...])
    scale = HEAD_DIM ** -0.5
    for j in range(N_HEADS // 2):
        s = rope(q[:, j * LANES:(j + 1) * LANES]) * scale
        q_ref[0, 2 * j] = s[:, :HEAD_DIM].astype(BF16)
        q_ref[0, 2 * j + 1] = s[:, HEAD_DIM:].astype(BF16)

    kv = _dot(hb, wkv_ref[...])

    def seg(i):
        return kv[:, i * KV_DIM:(i + 1) * KV_DIM]

    kc_ref[...] = roped(seg(0)).astype(BF16)
    vc_ref[...] = seg(1).astype(BF16)
    kst_ref[0] = roped(seg(2)).T.astype(BF16)
    vs = seg(3)
    kwt_ref[0] = roped(seg(4)).T.astype(BF16)
    vw = seg(5)
    ones_col = jnp.where(lax.broadcasted_iota(jnp.int32, (x.shape[0], LANES - HEAD_DIM), 1) == 0, 1.0, 0.0)
    for hh in range(N_KV_HEADS):
        sl = slice(hh * HEAD_DIM, (hh + 1) * HEAD_DIM)
        vs_ref[0, hh] = jnp.concatenate([vs[:, sl], ones_col], axis=1).astype(BF16)
        vw_ref[0, hh] = jnp.concatenate([vw[:, sl], ones_col], axis=1).astype(BF16)

    sg = jax.nn.sigmoid(_dot(hb, wg_ref[...]))
    gt_ref[0] = sg
    for hh in range(1, N_KV_HEADS):
        gt_ref[hh] = pltpu.roll(sg, LANES - 3 * GQA * hh, 1)


def _input_proj(x2, g, sh, sc, w_in, rope_tabs, bsz, seq, tm):
    t, d = x2.shape
    d_ssm = d // 2
    d_att = d - d_ssm
    nt = seq // tm
    w_in = w_in.astype(BF16)
    o = 0
    wu = w_in[:, o:o + d_ssm]; o += d_ssm
    wq = w_in[:, o:o + d_att]; o += d_att
    wkv = w_in[:, o:o + 6 * KV_DIM]; o += 6 * KV_DIM
    wg = jnp.zeros((d, LANES), BF16).at[:, :3 * N_HEADS].set(w_in[:, o:])
    row = lambda i: (i, 0)
    bat = lambda i: (i // nt, 0, 0)
    tab = lambda i: (i % nt, 0)
    hm = lambda i: (i // nt, 0, i % nt, 0)
    tr = lambda i: (i // nt, 0, i % nt)
    in_specs = [pl.BlockSpec((tm, d), row), _const_spec((1, d)),
                pl.BlockSpec((1, 1, d), bat), pl.BlockSpec((1, 1, d), bat),
                _const_spec(wu.shape), _const_spec(wq.shape), _const_spec(wkv.shape), _const_spec(wg.shape),
                pl.BlockSpec((tm, LANES), tab), pl.BlockSpec((tm, LANES), tab), pl.BlockSpec((tm, LANES), tab)]
    out_shape = [jax.ShapeDtypeStruct((t, d_ssm), BF16),
                 jax.ShapeDtypeStruct((bsz, N_HEADS, seq, HEAD_DIM), BF16),
                 jax.ShapeDtypeStruct((t, KV_DIM), BF16),
                 jax.ShapeDtypeStruct((t, KV_DIM), BF16),
                 jax.ShapeDtypeStruct((bsz, KV_DIM, seq), BF16),
                 jax.ShapeDtypeStruct((bsz, N_KV_HEADS, seq, LANES), BF16),
                 jax.ShapeDtypeStruct((bsz, KV_DIM, seq), BF16),
                 jax.ShapeDtypeStruct((bsz, N_KV_HEADS, seq, LANES), BF16),
                 jax.ShapeDtypeStruct((N_KV_HEADS, t, LANES), F32)]
    out_specs = [pl.BlockSpec((tm, d_ssm), row),
                 pl.BlockSpec((1, N_HEADS, tm, HEAD_DIM), hm),
                 pl.BlockSpec((tm, KV_DIM), row),
                 pl.BlockSpec((tm, KV_DIM), row),
                 pl.BlockSpec((1, KV_DIM, tm), tr),
                 pl.BlockSpec((1, N_KV_HEADS, tm, LANES), hm),
                 pl.BlockSpec((1, KV_DIM, tm), tr),
                 pl.BlockSpec((1, N_KV_HEADS, tm, LANES), hm),
                 pl.BlockSpec((N_KV_HEADS, tm, LANES), lambda i: (0, i, 0))]
    return pl.pallas_call(
        _proj_kernel,
        grid=(t // tm,),
        in_specs=in_specs, out_specs=out_specs, out_shape=out_shape,
        compiler_params=_params(("arbitrary",)),
        name="input_proj",
    )(x2, g.reshape(1, d), sh.reshape(bsz, 1, d), sc.reshape(bsz, 1, d), wu, wq, wkv, wg, *rope_tabs)


def _s5prep_kernel(lr_ref, li_ref, ldt_ref, btr_ref, bti_ref, cr_ref, ci_ref, dsk_ref,
                   m_ref, bzr_ref, bzi_ref, cyr_ref, cyi_ref, ar_ref, ai_ref):
    L, H, P = S5_CHUNK, SSM_GROUP, SSM_STATE
    lr = lr_ref[0]
    li = li_ref[0]
    dt = jnp.exp(ldt_ref[0])
    mag = jnp.exp(lr * dt)
    ab_re = mag * jnp.cos(li * dt)
    ab_im = mag * jnp.sin(li * dt)
    den = lr * lr + li * li
    num_re = ab_re - 1.0
    f_re = (num_re * lr + ab_im * li) / den
    f_im = (ab_im * lr - num_re * li) / den
    btr = btr_ref[0]
    bti = bti_ref[0]
    bb_re = f_re * btr - f_im * bti
    bb_im = f_re * bti + f_im * btr
    c_re = cr_ref[0]
    c_im = ci_ref[0]

    r = lax.broadcasted_iota(jnp.int32, (4 * L, 1), 0)
    k = r & (L - 1)
    blk = r >> 4
    e = jnp.where(blk == 0, -k, jnp.where(blk == 1, k, jnp.where(blk == 2, L - 1 - k, k + 1))).astype(F32)
    pm = jnp.exp(lr * dt * e)
    ph = li * dt * e
    pr = pm * jnp.cos(ph)
    pi = pm * jnp.sin(ph)

    def rows(i, sm_r, sm_i):
        a_r = jnp.broadcast_to(pr[i * L:(i + 1) * L][:, None, :], (L, H, P)).reshape(L * H, P)
        a_i = jnp.broadcast_to(pi[i * L:(i + 1) * L][:, None, :], (L, H, P)).reshape(L * H, P)
        s_r = jnp.broadcast_to(sm_r[None], (L, H, P)).reshape(L * H, P)
        s_i = jnp.broadcast_to(sm_i[None], (L, H, P)).reshape(L * H, P)
        return a_r * s_r - a_i * s_i, a_r * s_i + a_i * s_r

    x_re, x_im = rows(0, bb_re, bb_im)
    y_re, y_im = rows(1, c_re, c_im)
    z_re, z_im = rows(2, bb_re, bb_im)
    w_re, w_im = rows(3, c_re, c_im)
    mm = _dot_nt(x_re, y_re, HIGHEST) - _dot_nt(x_im, y_im, HIGHEST)
    ri = lax.broadcasted_iota(jnp.int32, (L * H, L * H), 0)
    ci = lax.broadcasted_iota(jnp.int32, (L * H, L * H), 1)
    mm = jnp.where((ci >> 4) >= (ri >> 4), mm, 0.0) + jnp.where(ri == ci, dsk_ref[0], 0.0)
    m_ref[0] = mm.astype(BF16)
    bzr_ref[0] = z_re.astype(BF16)
    bzi_ref[0] = z_im.astype(BF16)
    cyr_ref[0] = w_re.astype(BF16)
    cyi_ref[0] = (-w_im).astype(BF16)
    m16 = jnp.exp(lr * dt * L)
    ar_ref[0] = m16 * jnp.cos(li * dt * L)
    ai_ref[0] = m16 * jnp.sin(li * dt * L)


def _s5_prep(lam_re, lam_im, log_dt, b_re, b_im, c_re, c_im, d_skip):
    g, p, h = b_re.shape
    lh = S5_CHUNK * h
    g3 = lambda i: (i, 0, 0)
    ins = [lam_re.reshape(g, 1, p), lam_im.reshape(g, 1, p),
           jnp.broadcast_to(log_dt.reshape(g, 1, 1), (g, 1, p)),
           b_re.transpose(0, 2, 1), b_im.transpose(0, 2, 1), c_re, c_im,
           jnp.tile(d_skip, (1, S5_CHUNK)).reshape(g, 1, lh)]
    in_specs = [pl.BlockSpec((1,) + a.shape[1:], g3) for a in ins]
    out_shape = [jax.ShapeDtypeStruct((g, lh, lh), BF16)] + \
                [jax.ShapeDtypeStruct((g, lh, p), BF16)] * 4 + \
                [jax.ShapeDtypeStruct((g, 1, p), F32)] * 2
    out_specs = [pl.BlockSpec((1,) + s.shape[1:], g3) for s in out_shape]
    return pl.pallas_call(
        _s5prep_kernel, grid=(g,), in_specs=in_specs, out_specs=out_specs, out_shape=out_shape,
        compiler_params=_params(("arbitrary",)), name="s5_prep",
    )(*ins)


def _s5_kernel(u_ref, m_ref, bzr_ref, bzi_ref, cyr_ref, cyi_ref, ar_ref, ai_ref, y_ref,
               yi_sc, zr_sc, zi_sc, xr_sc, xi_sc, *, gb, nch, bsz):
    P = SSM_STATE
    for g in range(gb):
        ug = u_ref[g]
        yi_sc[g] = _dot(ug, m_ref[g])
        zr_sc[:, g * P:(g + 1) * P] = _dot(ug, bzr_ref[g])
        zi_sc[:, g * P:(g + 1) * P] = _dot(ug, bzi_ref[g])
    a_r = ar_ref[0]
    a_i = ai_ref[0]

    def body(c, carry):
        new = []
        for b in range(bsz):
            x_r, x_i = carry[2 * b], carry[2 * b + 1]
            row = b * nch + c
            xr_sc[pl.ds(row, 1), :] = x_r
            xi_sc[pl.ds(row, 1), :] = x_i
            z_r = zr_sc[pl.ds(row, 1), :]
            z_i = zi_sc[pl.ds(row, 1), :]
            new.append(a_r * x_r - a_i * x_i + z_r)
            new.append(a_r * x_i + a_i * x_r + z_i)
        return tuple(new)

    zero = jnp.zeros((1, gb * P), F32)
    lax.fori_loop(0, nch, body, (zero,) * (2 * bsz))
    for g in range(gb):
        x_r = xr_sc[:, g * P:(g + 1) * P].astype(BF16)
        x_i = xi_sc[:, g * P:(g + 1) * P].astype(BF16)
        y = yi_sc[g] + _dot_nt(x_r, cyr_ref[g]) + _dot_nt(x_i, cyi_ref[g])
        y_ref[g] = y.astype(BF16)


def _s5_scan(u_g, ops, bsz):
    m, bzr, bzi, cyr, cyi, a_r, a_i = ops
    g, rows, lh = u_g.shape
    p = SSM_STATE
    gb = S5_GROUPS_PER_STEP
    nch = rows // bsz
    a_r = a_r.reshape(g // gb, 1, gb * p)
    a_i = a_i.reshape(g // gb, 1, gb * p)
    g3 = lambda i: (i, 0, 0)
    in_specs = [pl.BlockSpec((gb, rows, lh), g3), pl.BlockSpec((gb, lh, lh), g3)] + \
               [pl.BlockSpec((gb, lh, p), g3)] * 4 + [pl.BlockSpec((1, 1, gb * p), g3)] * 2
    return pl.pallas_call(
        functools.partial(_s5_kernel, gb=gb, nch=nch, bsz=bsz),
        grid=(g // gb,),
        in_specs=in_specs,
        out_specs=pl.BlockSpec((gb, rows, lh), g3),
        out_shape=jax.ShapeDtypeStruct((g, rows, lh), BF16),
        scratch_shapes=[pltpu.VMEM((gb, rows, lh), F32)] + [pltpu.VMEM((rows, gb * p), F32)] * 4,
        compiler_params=_params(("arbitrary",)),
        name="s5_scan",
    )(u_g, m, bzr, bzi, cyr, cyi, a_r, a_i)


def _cmp_kernel(tok_ref, w1x_ref, pe_ref, w1_ref, w2x_ref, o_ref, *, transpose_out):
    nh = N_KV_HEADS * CMP_HIDDEN
    g = _dot(tok_ref[...], w1x_ref[...])
    rows = g.shape[0]
    bias = jnp.dot(jnp.broadcast_to(pe_ref[...], (8, pe_ref.shape[1])), w1_ref[...],
                   preferred_element_type=F32, precision=HIGHEST)[0:1]
    bias = jnp.concatenate([bias] * N_KV_HEADS, axis=1)
    hid = g[:, :nh] + pltpu.roll(g[:, nh:], rows - 1, 0) + bias
    out = _dot(jax.nn.gelu(hid).astype(BF16), w2x_ref[...])
    if transpose_out:
        o_ref[0] = out.T.astype(BF16)
    else:
        for hh in range(N_KV_HEADS):
            o_ref[0, hh] = out[:, hh * HEAD_DIM:(hh + 1) * HEAD_DIM].astype(BF16)


def _compress(tok, pe, w1, w2, bsz, seq, transpose_out):
    ncp = seq // CMP_STRIDE
    half = CMP_STRIDE * HEAD_DIM
    eye = jnp.eye(N_KV_HEADS, dtype=F32)
    w1r = w1.reshape(2, CMP_STRIDE, HEAD_DIM, CMP_HIDDEN)
    w1x = jnp.einsum('zkdj,hg->khdzgj', w1r, eye).reshape(CMP_STRIDE * KV_DIM, 2 * N_KV_HEADS * CMP_HIDDEN)
    w2x = jnp.einsum('jd,hg->hjgd', w2, eye).reshape(N_KV_HEADS * CMP_HIDDEN, KV_DIM)
    tok2 = tok.reshape(bsz * ncp, CMP_STRIDE * KV_DIM)
    if transpose_out:
        out_shape = jax.ShapeDtypeStruct((bsz, KV_DIM, ncp), BF16)
        out_spec = pl.BlockSpec((1, KV_DIM, ncp), lambda b: (b, 0, 0))
    else:
        out_shape = jax.ShapeDtypeStruct((bsz, N_KV_HEADS, ncp, HEAD_DIM), BF16)
        out_spec = pl.BlockSpec((1, N_KV_HEADS, ncp, HEAD_DIM), lambda b: (b, 0, 0, 0))
    return pl.pallas_call(
        functools.partial(_cmp_kernel, transpose_out=transpose_out),
        grid=(bsz,),
        in_specs=[pl.BlockSpec((ncp, CMP_STRIDE * KV_DIM), lambda b: (b, 0)),
                  _const_spec(w1x.shape), _const_spec((1, 2 * half)), _const_spec(w1.shape),
                  _const_spec(w2x.shape)],
        out_specs=out_spec, out_shape=out_shape,
        compiler_params=_params(("arbitrary",)),
        name="compress_k" if transpose_out else "compress_v",
    )(tok2, w1x.astype(BF16), pe.reshape(1, 2 * half), w1, w2x.astype(BF16))


def _nsa_kernel(q_ref, kct_ref, vc_ref, kst_ref, vs_ref, kwt_ref, vw_ref, gt_ref, amat_ref, o_ref,
                rhs_sc, lhs_sc, m_sc, st_sc, mw_sc, stw_sc, *, tq, seq):
    nblk = seq // SEL_BLOCK
    ncp = seq // CMP_STRIDE
    rows = GQA * tq
    nslab = tq // LANES
    qi = pl.program_id(2)
    t0 = qi * tq

    @pl.when(qi == 0)
    def _():
        rhs_sc[0:HEAD_DIM, :] = kst_ref[0]
        rhs_sc[HEAD_DIM:LANES, :] = jnp.zeros((LANES - HEAD_DIM, seq), BF16)
        cw = min(seq, 1024)

        def fill(c, _):
            col = c * cw + lax.broadcasted_iota(jnp.int32, (nblk, cw), 1)
            blk = lax.broadcasted_iota(jnp.int32, (nblk, cw), 0)
            onehot = jnp.where((col >> 6) == blk, 1.0, 0.0).astype(BF16)
            rhs_sc[LANES:LANES + nblk, pl.ds(pl.multiple_of(c * cw, cw), cw)] = onehot
            return 0

        lax.fori_loop(0, seq // cw, fill, 0)

    t_q = t0 + lax.broadcasted_iota(jnp.int32, (tq, 1), 0)

    def reset(state):
        state[0][...] = jnp.full((rows, LANES), NEG_BIG, F32)
        state[1][...] = jnp.zeros((rows, LANES), F32)

    def online(state, g, s, v_aug):
        m_ref, st_ref = state
        sl = slice(g * tq, (g + 1) * tq)
        m_old = m_ref[sl, :]
        mx = s[:, 0:LANES]
        for j in range(1, nslab):
            mx = jnp.maximum(mx, s[:, j * LANES:(j + 1) * LANES])
        m_new = jnp.maximum(m_old, jnp.max(mx, axis=-1, keepdims=True))
        alpha = jnp.exp(m_old - m_new)
        p = jnp.concatenate([jnp.exp(s[:, j * LANES:(j + 1) * LANES] - m_new) for j in range(nslab)], axis=1)
        st_ref[sl, :] = alpha * st_ref[sl, :] + _dot(p.astype(BF16), v_aug)
        m_ref[sl, :] = m_new

    def result(state, g):
        st = state[1][g * tq:(g + 1) * tq, :]
        return st[:, 0:HEAD_DIM] * (1.0 / st[:, HEAD_DIM:HEAD_DIM + 1])

    def kpos(kj):
        return kj * tq + lax.broadcasted_iota(jnp.int32, (1, tq), 1)

    win_st = (mw_sc, stw_sc)
    reset(win_st)
    for w in range(WINDOW // tq + 1):
        kj = qi - WINDOW // tq + w
        off = pl.multiple_of(jnp.maximum(kj, 0) * tq, tq)
        kt = kwt_ref[0, :, pl.ds(off, tq)]
        v_aug = vw_ref[0, 0, pl.ds(off, tq), :]
        kp = kpos(kj)
        mask = (kp <= t_q) & (kp > t_q - WINDOW) & (kp >= 0)
        for g in range(GQA):
            online(win_st, g, jnp.where(mask, _dot(q_ref[0, g], kt), NEG_BIG), v_aug)

    n_col = lax.broadcasted_iota(jnp.int32, (1, ncp), 1)
    valid = (n_col * CMP_STRIDE + (CMP_BLOCK - 1)) <= t_q
    kct = kct_ref[0]
    vc = vc_ref[0, 0]
    o_cmp = []
    psum = None
    for g in range(GQA):
        sm = jnp.where(valid, _dot(q_ref[0, g], kct), NEG_BIG)
        e = jnp.exp(sm - jnp.max(sm, axis=-1, keepdims=True))
        p = jnp.where(valid, e * (1.0 / jnp.sum(e, axis=-1, keepdims=True)), 0.0)
        o_cmp.append(_dot(p.astype(BF16), vc))
        psum = p if g == 0 else psum + p

    amat = amat_ref[...]
    p_hi = psum.astype(BF16)
    r1 = psum - p_hi.astype(F32)
    p_mid = r1.astype(BF16)
    p_lo = (r1 - p_mid.astype(F32)).astype(BF16)
    imp = _dot(p_hi, amat) + _dot(p_mid, amat) + _dot(p_lo, amat)
    blk = lax.broadcasted_iota(jnp.int32, (nblk, 1), 0)
    t_l = t0 + lax.broadcasted_iota(jnp.int32, (1, tq), 1)
    cur = t_l >> 6
    forced = (blk == 0) | (blk == cur) | (blk == cur - 1)
    impm = jnp.where(forced, -NEG_BIG, jnp.where(blk * SEL_BLOCK <= t_l, imp.T, NEG_BIG))
    impm = impm.reshape(nblk // 8, 8, tq)
    blk_f = (lax.broadcasted_iota(jnp.int32, (nblk // 8, 8, tq), 0) * 8
             + lax.broadcasted_iota(jnp.int32, (nblk // 8, 8, tq), 1)).astype(F32)

    def over_blocks(x, op):
        r = x[0]
        for a in range(1, nblk // 8):
            r = op(r, x[a])
        for sh in (4, 2, 1):
            r = op(r, pltpu.roll(r, sh, 0))
        return r[None]

    selneg = jnp.full((nblk // 8, 8, tq), NEG_BIG, F32)
    for _ in range(min(N_SELECT, nblk)):
        best = over_blocks(impm, jnp.maximum)
        idx = over_blocks(jnp.where(impm == best, blk_f, float(nblk)), jnp.minimum)
        pick = blk_f == idx
        selneg = jnp.where(pick, 0.0, selneg)
        impm = jnp.where(pick, -jnp.inf, impm)
    selneg = selneg.reshape(nblk, tq).T.astype(BF16)

    for g in range(GQA):
        sl = slice(g * tq, (g + 1) * tq)
        lhs_sc[sl, 0:HEAD_DIM] = q_ref[0, g]
        lhs_sc[sl, HEAD_DIM:LANES] = jnp.zeros((tq, LANES - HEAD_DIM), BF16)
        lhs_sc[sl, LANES:LANES + nblk] = selneg

    sel_st = (m_sc, st_sc)
    reset(sel_st)

    def sel_scores(kj):
        rhs = rhs_sc[:, pl.ds(pl.multiple_of(kj * tq, tq), tq)]
        return tuple(_dot(lhs_sc[g * tq:(g + 1) * tq, :], rhs) for g in range(GQA))

    def sel_update(kj, scores, mask):
        v_aug = vs_ref[0, 0, pl.ds(pl.multiple_of(kj * tq, tq), tq), :]
        for g in range(GQA):
            online(sel_st, g, scores[g] if mask is None else jnp.where(mask, scores[g], NEG_BIG), v_aug)

    def sel_step(kj, scores):
        nxt = sel_scores(kj + 1)
        sel_update(kj, scores, None)
        return nxt

    scores = lax.fori_loop(0, qi, sel_step, sel_scores(0))
    sel_update(qi, scores, kpos(qi) <= t_q)

    gt = gt_ref[0]
    for g in range(GQA):
        o = (gt[:, 3 * g:3 * g + 1] * o_cmp[g] + gt[:, 3 * g + 1:3 * g + 2] * result(sel_st, g)
             + gt[:, 3 * g + 2:3 * g + 3] * result(win_st, g))
        o_ref[:, g * HEAD_DIM:(g + 1) * HEAD_DIM] = o


def _importance_matrix(seq):
    ncp = seq // CMP_STRIDE
    nblk = seq // SEL_BLOCK
    ratio = SEL_BLOCK // CMP_STRIDE
    n = jnp.arange(ncp)[:, None]
    j = jnp.arange(nblk)[None, :]
    a = sum(((n == ratio * j + r).astype(F32) + (n == ratio * j + r + 1).astype(F32)) for r in range(ratio))
    return jnp.where(n < ncp - 1, a, 0.0).astype(BF16)


def _nsa_attention(q_hm, kct, vc_hm, kst, vs_hm, kwt, vw_hm, gates, bsz, seq):
    tq = min(ATT_TILE, seq)
    nq = seq // tq
    nblk = seq // SEL_BLOCK
    ncp = seq // CMP_STRIDE
    rows = GQA * tq
    amat = _importance_matrix(seq)
    head = lambda b, h, i: (b, h, 0, 0)
    headt = lambda b, h, i: (b, h, 0)
    in_specs = [pl.BlockSpec((1, GQA, tq, HEAD_DIM), lambda b, h, i: (b, h, i, 0)),
                pl.BlockSpec((1, HEAD_DIM, ncp), headt),
                pl.BlockSpec((1, 1, ncp, HEAD_DIM), head),
                pl.BlockSpec((1, HEAD_DIM, seq), headt),
                pl.BlockSpec((1, 1, seq, LANES), head),
                pl.BlockSpec((1, HEAD_DIM, seq), headt),
                pl.BlockSpec((1, 1, seq, LANES), head),
                pl.BlockSpec((1, tq, LANES), lambda b, h, i: (h, b * nq + i, 0)),
                _const_spec(amat.shape)]
    return pl.pallas_call(
        functools.partial(_nsa_kernel, tq=tq, seq=seq),
        grid=(bsz, N_KV_HEADS, nq),
        in_specs=in_specs,
        out_specs=pl.BlockSpec((tq, GQA * HEAD_DIM), lambda b, h, i: (b * nq + i, h)),
        out_shape=jax.ShapeDtypeStruct((bsz * seq, N_HEADS * HEAD_DIM), F32),
        scratch_shapes=[pltpu.VMEM((LANES + nblk, seq), BF16),
                        pltpu.VMEM((rows, LANES + nblk), BF16),
                        pltpu.VMEM((rows, LANES), F32), pltpu.VMEM((rows, LANES), F32),
                        pltpu.VMEM((rows, LANES), F32), pltpu.VMEM((rows, LANES), F32)],
        compiler_params=_params(("arbitrary", "arbitrary", "arbitrary")),
        name="nsa_attention",
    )(q_hm, kct, vc_hm, kst, vs_hm, kwt, vw_hm, gates, amat)


def _out_kernel(ys_ref, ya_ref, x_ref, wglu_ref, bglu_ref, gs_ref, ga_ref, woa_ref, wob_ref,
                ga1_ref, gf_ref, sh2_ref, sc2_ref, x1_ref, h2_ref):
    z = jax.nn.gelu(ys_ref[...].astype(F32))
    glu = z * jax.nn.sigmoid(_dot(z.astype(BF16), wglu_ref[...]) + bglu_ref[...])
    n1 = _rms(glu, gs_ref[...]).astype(BF16)
    n2 = _rms(ya_ref[...], ga_ref[...]).astype(BF16)
    y = _dot(n1, woa_ref[...]) + _dot(n2, wob_ref[...])
    x1 = x_ref[...] + ga1_ref[0] * y
    x1_ref[...] = x1
    h2_ref[...] = (_rms(x1, gf_ref[...]) * (1.0 + sc2_ref[0]) + sh2_ref[0]).astype(BF16)


def _out_proj(y_ssm, y_att, x2, w_glu, b_glu, g_ssm, g_nsa, w_out, ga1, g_ffn, sh2, sc2, bsz, seq, tm):
    t, d = x2.shape
    d_ssm = y_ssm.shape[1]
    d_att = y_att.shape[1]
    nt = seq // tm
    row = lambda i: (i, 0)
    bat = lambda i: (i // nt, 0, 0)
    w_out = w_out.astype(BF16)
    in_specs = [pl.BlockSpec((tm, d_ssm), row), pl.BlockSpec((tm, d_att), row), pl.BlockSpec((tm, d), row),
                _const_spec((d_ssm, d_ssm)), _const_spec((1, d_ssm)), _const_spec((1, d_ssm)),
                _const_spec((1, d_att)), _const_spec((d_ssm, d)), _const_spec((d_att, d)),
                pl.BlockSpec((1, 1, d), bat), _const_spec((1, d)),
                pl.BlockSpec((1, 1, d), bat), pl.BlockSpec((1, 1, d), bat)]
    return pl.pallas_call(
        _out_kernel,
        grid=(t // tm,),
        in_specs=in_specs,
        out_specs=[pl.BlockSpec((tm, d), row), pl.BlockSpec((tm, d), row)],
        out_shape=[jax.ShapeDtypeStruct((t, d), F32), jax.ShapeDtypeStruct((t, d), BF16)],
        compiler_params=_params(("arbitrary",)),
        name="out_proj",
    )(y_ssm, y_att, x2, w_glu.astype(BF16), b_glu.reshape(1, d_ssm), g_ssm.reshape(1, d_ssm),
      g_nsa.reshape(1, d_att), w_out[:d_ssm], w_out[d_ssm:], ga1.reshape(bsz, 1, d), g_ffn.reshape(1, d),
      sh2.reshape(bsz, 1, d), sc2.reshape(bsz, 1, d))


FFN_HALO = 16


def _ffn_kernel(h_ref, halo_ref, x1_ref, wv_ref, wg_ref, cwv_ref, cwg_ref, cbv_ref, cbg_ref, wd_ref,
                ga2_ref, gfin_ref, o_ref, *, nt):
    i = pl.program_id(0)
    j = pl.program_id(1)
    tm = h_ref.shape[0]
    halo = halo_ref[...]
    halo = jnp.where(i % nt == 0, jnp.zeros_like(halo), halo)
    hx = jnp.concatenate([halo, h_ref[...]], axis=0)

    def conv(w_ref, cw_ref, cb_ref):
        up = _dot(hx, w_ref[...])
        cw = cw_ref[...]
        c = (pltpu.roll(up, 2, 0) * cw[0:1] + pltpu.roll(up, 1, 0) * cw[1:2] + up * cw[2:3])
        return c[FFN_HALO:] + cb_ref[...]

    val = conv(wv_ref, cwv_ref, cbv_ref)
    gate = conv(wg_ref, cwg_ref, cbg_ref)
    act = (gate * jax.nn.sigmoid(gate) * val).astype(BF16)
    part = _dot(act, wd_ref[...])

    @pl.when(j == 0)
    def _():
        o_ref[...] = part

    @pl.when(j > 0)
    def _():
        o_ref[...] += part

    @pl.when(j == pl.num_programs(1) - 1)
    def _():
        x2 = x1_ref[...] + ga2_ref[0] * o_ref[...]
        o_ref[...] = _rms(x2, gfin_ref[...])


def _conv_ffn(h2, x1, w_up, conv_w, conv_b, w_down, ga2, g_final, bsz, seq, tm, tn):
    t, d = x1.shape
    dff = w_down.shape[0]
    nt = seq // tm
    nj = dff // tn
    w_up = w_up.astype(BF16)
    w_down = w_down.astype(BF16)
    cb = conv_b.reshape(1, 2 * dff)
    hb = tm // FFN_HALO
    in_specs = [pl.BlockSpec((tm, d), lambda i, j: (i, 0)),
                pl.BlockSpec((FFN_HALO, d), lambda i, j: (jnp.maximum(i * hb - 1, 0), 0)),
                pl.BlockSpec((tm, d), lambda i, j: (i, 0)),
                pl.BlockSpec((d, tn), lambda i, j: (0, j)),
                pl.BlockSpec((d, tn), lambda i, j: (0, nj + j)),
                pl.BlockSpec((CONV_WIDTH, tn), lambda i, j: (0, j)),
                pl.BlockSpec((CONV_WIDTH, tn), lambda i, j: (0, nj + j)),
                pl.BlockSpec((1, tn), lambda i, j: (0, j)),
                pl.BlockSpec((1, tn), lambda i, j: (0, nj + j)),
                pl.BlockSpec((tn, d), lambda i, j: (j, 0)),
                pl.BlockSpec((1, 1, d), lambda i, j: (i // nt, 0, 0)),
                pl.BlockSpec((1, d), lambda i, j: (0, 0))]
    return pl.pallas_call(
        functools.partial(_ffn_kernel, nt=nt),
        grid=(t // tm, nj),
        in_specs=in_specs,
        out_specs=pl.BlockSpec((tm, d), lambda i, j: (i, 0)),
        out_shape=jax.ShapeDtypeStruct((t, d), F32),
        compiler_params=_params(("arbitrary", "arbitrary")),
        name="conv_ffn",
    )(h2, h2, x1, w_up, w_up, conv_w, conv_w, cb, cb, w_down, ga2.reshape(bsz, 1, d), g_final.reshape(1, d))


def _block(x, mod, g_mix_norm, w_in, s5_params, cmp_k, cmp_v, g_ssm_out, g_nsa_out, w_glu, b_glu, w_out,
           g_ffn_norm, w_up, conv_w, conv_b, w_down, g_final):
    bsz, seq, d = x.shape
    t = bsz * seq
    d_ssm = d // 2
    x2 = x.reshape(t, d)
    sh1, sc1, ga1, sh2, sc2, ga2 = jnp.split(mod, 6, axis=-1)
    tm = min(512, seq)

    u, q_hm, kc_tok, vc_tok, kst, vs_hm, kwt, vw_hm, gates = _input_proj(
        x2, g_mix_norm, sh1, sc1, w_in, _rope_tables(seq), bsz, seq, tm)

    ng = d_ssm // SSM_GROUP
    u_g = u.reshape(t // S5_CHUNK, S5_CHUNK, ng, SSM_GROUP).transpose(2, 0, 1, 3).reshape(
        ng, t // S5_CHUNK, S5_CHUNK * SSM_GROUP)
    y_g = _s5_scan(u_g, _s5_prep(*s5_params), bsz)
    y_ssm = y_g.reshape(ng, t // S5_CHUNK, S5_CHUNK, SSM_GROUP).transpose(1, 2, 0, 3).reshape(t, d_ssm)

    kct = _compress(kc_tok, *cmp_k, bsz, seq, True)
    vc_hm = _compress(vc_tok, *cmp_v, bsz, seq, False)
    y_att = _nsa_attention(q_hm, kct, vc_hm, kst, vs_hm, kwt, vw_hm, gates, bsz, seq)

    x1, h2 = _out_proj(y_ssm, y_att, x2, w_glu, b_glu, g_ssm_out, g_nsa_out, w_out, ga1, g_ffn_norm,
                       sh2, sc2, bsz, seq, tm)
    out = _conv_ffn(h2, x1, w_up, conv_w, conv_b, w_down, ga2, g_final, bsz, seq, tm, 512)
    return out.reshape(bsz, seq, d)


def kernel(x, c, w_ada, b_ada, g_mix_norm, w_in, lam_re, lam_im, log_dt, b_re, b_im, c_re, c_im, d_skip,
           w_glu, b_glu, pe_k, w1_k, w2_k, pe_v, w1_v, w2_v, g_ssm_out, g_nsa_out, w_out, g_ffn_norm,
           w_up, conv_w, conv_b, w_down, g_final):
    l = 0
    mod = _adaln_mod(c, w_ada[l], b_ada[l])
    return _block(x, mod, g_mix_norm[l], w_in[l],
                  (lam_re[l], lam_im[l], log_dt[l], b_re[l], b_im[l], c_re[l], c_im[l], d_skip[l]),
                  (pe_k[l], w1_k[l], w2_k[l]), (pe_v[l], w1_v[l], w2_v[l]),
                  g_ssm_out[l], g_nsa_out[l], w_glu[l], b_glu[l], w_out[l], g_ffn_norm[l],
                  w_up[l], conv_w[l], conv_b[l], w_down[l], g_final)
```

```python
import functools
import math

import jax
import jax.numpy as jnp
from jax import lax
from jax.experimental import pallas as pl
from jax.experimental.pallas import tpu as pltpu

F32 = jnp.float32
BF16 = jnp.bfloat16
HIGHEST = lax.Precision.HIGHEST

SSM_GROUP = 16
SSM_STATE = 64
N_HEADS = 16
N_KV_HEADS = 4
HEAD_DIM = 64
GQA = N_HEADS // N_KV_HEADS
KV_DIM = N_KV_HEADS * HEAD_DIM
ROT_DIM = HEAD_DIM // 4
ROPE_THETA = 500000.0
CMP_BLOCK = 32
CMP_STRIDE = 16
CMP_HIDDEN = 2 * HEAD_DIM
SEL_BLOCK = 64
N_SELECT = 16
WINDOW = 512
CONV_WIDTH = 3
NORM_EPS = 1e-6
NEG_BIG = -1e30

LANES = 128
S5_CHUNK = 16
ATT_TILE = 256
ATT_SUB_ROWS = 256
VMEM_LIMIT = 56 * 2**20


def _params(sem, vmem=VMEM_LIMIT):
    return pltpu.CompilerParams(dimension_semantics=sem, vmem_limit_bytes=vmem)


def _const_spec(shape):
    n = len(shape)
    return pl.BlockSpec(shape, lambda *_: (0,) * n)


def _rms(x, g):
    return x * lax.rsqrt(jnp.mean(x * x, axis=-1, keepdims=True) + NORM_EPS) * g


def _dot(a, b):
    return jnp.dot(a, b, preferred_element_type=F32)


def _dot_nt(a, b, precision=None):
    return lax.dot_general(a, b, (((1,), (1,)), ((), ())), precision=precision,
                           preferred_element_type=F32)


def _mod_kernel(c_ref, w_ref, b_ref, o_ref):
    c = c_ref[...]
    sc = c * jax.nn.sigmoid(c)
    o_ref[...] = jnp.dot(sc, w_ref[...], preferred_element_type=F32, precision=HIGHEST) + b_ref[...]


def _adaln_mod(c, w_ada, b_ada):
    bsz, d = c.shape
    n = w_ada.shape[1]
    tn = n // 8
    cp = jnp.zeros((8, d), F32).at[:bsz].set(c)
    out = pl.pallas_call(
        _mod_kernel,
        grid=(n // tn,),
        in_specs=[pl.BlockSpec((8, d), lambda j: (0, 0)),
                  pl.BlockSpec((d, tn), lambda j: (0, j)),
                  pl.BlockSpec((1, tn), lambda j: (0, j))],
        out_specs=pl.BlockSpec((8, tn), lambda j: (0, j)),
        out_shape=jax.ShapeDtypeStruct((8, n), F32),
        compiler_params=_params(("arbitrary",)),
        name="adaln_mod",
    )(cp, w_ada, b_ada.reshape(1, n))
    return out[:bsz]


def _rope_kernel(c_ref, sm_ref, sp_ref, *, tr):
    i = pl.program_id(0)
    pos = (i * tr + lax.broadcasted_iota(jnp.int32, (tr, LANES), 0)).astype(F32)
    lane = lax.broadcasted_iota(jnp.int32, (tr, LANES), 1)
    d = lane & (HEAD_DIM - 1)
    half = ROT_DIM // 2
    fi = d & (half - 1)
    inv = jnp.zeros((tr, LANES), F32)
    for k in range(half):
        inv = jnp.where(fi == k, ROPE_THETA ** (-k / half), inv)
    ang = pos * inv
    cs = jnp.cos(ang)
    sn = jnp.sin(ang)
    c_ref[...] = jnp.where(d < ROT_DIM, cs, 1.0)
    sm_ref[...] = jnp.where(d < half, -sn, 0.0)
    sp_ref[...] = jnp.where((d >= half) & (d < ROT_DIM), sn, 0.0)


def _rope_tables(seq):
    tr = min(seq, 1024)
    sds = jax.ShapeDtypeStruct((seq, LANES), F32)
    spec = pl.BlockSpec((tr, LANES), lambda i: (i, 0))
    return pl.pallas_call(
        functools.partial(_rope_kernel, tr=tr),
        grid=(seq // tr,),
        in_specs=[],
        out_specs=[spec, spec, spec],
        out_shape=[sds, sds, sds],
        compiler_params=_params(("arbitrary",)),
        name="rope_tables",
    )()


def _proj_kernel(x_ref, g_ref, sh_ref, sc_ref, wu_ref, wq_ref, wkv_ref, wg_ref, rc_ref, rm_ref, rp_ref,
                 u_ref, q_ref, kc_ref, vc_ref, kst_ref, vs_ref, kwt_ref, vw_ref, gt_ref, usc):
    x = x_ref[...]
    h = _rms(x, g_ref[...]) * (1.0 + sc_ref[0]) + sh_ref[0]
    hb = h.astype(BF16)
    rc, rm, rp = rc_ref[...], rm_ref[...], rp_ref[...]

    def rope(s):
        return s * rc + pltpu.roll(s, LANES - ROT_DIM // 2, 1) * rm + pltpu.roll(s, ROT_DIM // 2, 1) * rp

    def roped(a):
        return jnp.concatenate([rope(a[:, :LANES]), rope(a[:, LANES:])], axis=1)

    u = _dot(hb, wu_ref[...])
    nchunk = x.shape[0] // S5_CHUNK
    for j in range(u_ref.shape[1]):
        usc[j] = u[:, j * LANES:(j + 1) * LANES]
        for k in range(S5_CHUNK):
            u_ref[k, j] = usc[j, pl.ds(k, nchunk, stride=S5_CHUNK), :].astype(BF16)

    q = _dot(hb, wq_ref[...])
    scale = HEAD_DIM ** -0.5
    for j in range(N_HEADS // 2):
        s = rope(q[:, j * LANES:(j + 1) * LANES]) * scale
        q_ref[0, 2 * j] = s[:, :HEAD_DIM].astype(BF16)
        q_ref[0, 2 * j + 1] = s[:, HEAD_DIM:].astype(BF16)

    kv = _dot(hb, wkv_ref[...])

    def seg(i):
        return kv[:, i * KV_DIM:(i + 1) * KV_DIM]

    kc_ref[...] = roped(seg(0)).astype(BF16)
    vc_ref[...] = seg(1).astype(BF16)
    kst_ref[0] = roped(seg(2)).T.astype(BF16)
    vs = seg(3)
    kwt_ref[0] = roped(seg(4)).T.astype(BF16)
    vw = seg(5)
    ones_col = jnp.where(lax.broadcasted_iota(jnp.int32, (x.shape[0], LANES - HEAD_DIM), 1) == 0, 1.0, 0.0)
    for hh in range(N_KV_HEADS):
        sl = slice(hh * HEAD_DIM, (hh + 1) * HEAD_DIM)
        vs_ref[0, hh] = jnp.concatenate([vs[:, sl], ones_col], axis=1).astype(BF16)
        vw_ref[0, hh] = jnp.concatenate([vw[:, sl], ones_col], axis=1).astype(BF16)

    sg = jax.nn.sigmoid(_dot(hb, wg_ref[...]))
    gt_ref[0] = sg
    for hh in range(1, N_KV_HEADS):
        gt_ref[hh] = pltpu.roll(sg, LANES - 3 * GQA * hh, 1)


def _input_proj(x2, g, sh, sc, w_in, rope_tabs, bsz, seq, tm):
    t, d = x2.shape
    d_ssm = d // 2
    d_att = d - d_ssm
    nt = seq // tm
    w_in = w_in.astype(BF16)
    o = 0
    wu = w_in[:, o:o + d_ssm]; o += d_ssm
    wq = w_in[:, o:o + d_att]; o += d_att
    wkv = w_in[:, o:o + 6 * KV_DIM]; o += 6 * KV_DIM
    wg = jnp.zeros((d, LANES), BF16).at[:, :3 * N_HEADS].set(w_in[:, o:])
    row = lambda i: (i, 0)
    bat = lambda i: (i // nt, 0, 0)
    tab = lambda i: (i % nt, 0)
    hm = lambda i: (i // nt, 0, i % nt, 0)
    tr = lambda i: (i // nt, 0, i % nt)
    in_specs = [pl.BlockSpec((tm, d), row), _const_spec((1, d)),
                pl.BlockSpec((1, 1, d), bat), pl.BlockSpec((1, 1, d), bat),
                _const_spec(wu.shape), _const_spec(wq.shape), _const_spec(wkv.shape), _const_spec(wg.shape),
                pl.BlockSpec((tm, LANES), tab), pl.BlockSpec((tm, LANES), tab), pl.BlockSpec((tm, LANES), tab)]
    out_shape = [jax.ShapeDtypeStruct((S5_CHUNK, d_ssm // LANES, t // S5_CHUNK, LANES), BF16),
                 jax.ShapeDtypeStruct((bsz, N_HEADS, seq, HEAD_DIM), BF16),
                 jax.ShapeDtypeStruct((t, KV_DIM), BF16),
                 jax.ShapeDtypeStruct((t, KV_DIM), BF16),
                 jax.ShapeDtypeStruct((bsz, KV_DIM, seq), BF16),
                 jax.ShapeDtypeStruct((bsz, N_KV_HEADS, seq, LANES), BF16),
                 jax.ShapeDtypeStruct((bsz, KV_DIM, seq), BF16),
                 jax.ShapeDtypeStruct((bsz, N_KV_HEADS, seq, LANES), BF16),
                 jax.ShapeDtypeStruct((N_KV_HEADS, t, LANES), F32)]
    out_specs = [pl.BlockSpec((S5_CHUNK, d_ssm // LANES, tm // S5_CHUNK, LANES), lambda i: (0, 0, i, 0)),
                 pl.BlockSpec((1, N_HEADS, tm, HEAD_DIM), hm),
                 pl.BlockSpec((tm, KV_DIM), row),
                 pl.BlockSpec((tm, KV_DIM), row),
                 pl.BlockSpec((1, KV_DIM, tm), tr),
                 pl.BlockSpec((1, N_KV_HEADS, tm, LANES), hm),
                 pl.BlockSpec((1, KV_DIM, tm), tr),
                 pl.BlockSpec((1, N_KV_HEADS, tm, LANES), hm),
                 pl.BlockSpec((N_KV_HEADS, tm, LANES), lambda i: (0, i, 0))]
    return pl.pallas_call(
        _proj_kernel,
        grid=(t // tm,),
        in_specs=in_specs, out_specs=out_specs, out_shape=out_shape,
        scratch_shapes=[pltpu.VMEM((d_ssm // LANES, tm, LANES), F32)],
        compiler_params=_params(("arbitrary",)),
        name="input_proj",
    )(x2, g.reshape(1, d), sh.reshape(bsz, 1, d), sc.reshape(bsz, 1, d), wu, wq, wkv, wg, *rope_tabs)


def _s5prep_kernel(lr_ref, li_ref, ldt_ref, btr_ref, bti_ref, cr_ref, ci_ref, dsk_ref,
                   m_ref, bzr_ref, bzi_ref, cyr_ref, cyi_ref, ar_ref, ai_ref):
    L, H, P = S5_CHUNK, SSM_GROUP, SSM_STATE
    lr = lr_ref[0]
    li = li_ref[0]
    dt = jnp.exp(ldt_ref[0])
    mag = jnp.exp(lr * dt)
    ab_re = mag * jnp.cos(li * dt)
    ab_im = mag * jnp.sin(li * dt)
    den = lr * lr + li * li
    num_re = ab_re - 1.0
    f_re = (num_re * lr + ab_im * li) / den
    f_im = (ab_im * lr - num_re * li) / den
    btr = btr_ref[0]
    bti = bti_ref[0]
    bb_re = f_re * btr - f_im * bti
    bb_im = f_re * bti + f_im * btr
    c_re = cr_ref[0]
    c_im = ci_ref[0]

    r = lax.broadcasted_iota(jnp.int32, (4 * L, 1), 0)
    k = r & (L - 1)
    blk = r >> 4
    e = jnp.where(blk == 0, -k, jnp.where(blk == 1, k, jnp.where(blk == 2, L - 1 - k, k + 1))).astype(F32)
    pm = jnp.exp(lr * dt * e)
    ph = li * dt * e
    pr = pm * jnp.cos(ph)
    pi = pm * jnp.sin(ph)

    def rows(i, sm_r, sm_i):
        a_r = jnp.broadcast_to(pr[i * L:(i + 1) * L][:, None, :], (L, H, P)).reshape(L * H, P)
        a_i = jnp.broadcast_to(pi[i * L:(i + 1) * L][:, None, :], (L, H, P)).reshape(L * H, P)
        s_r = jnp.broadcast_to(sm_r[None], (L, H, P)).reshape(L * H, P)
        s_i = jnp.broadcast_to(sm_i[None], (L, H, P)).reshape(L * H, P)
        return a_r * s_r - a_i * s_i, a_r * s_i + a_i * s_r

    x_re, x_im = rows(0, bb_re, bb_im)
    y_re, y_im = rows(1, c_re, c_im)
    z_re, z_im = rows(2, bb_re, bb_im)
    w_re, w_im = rows(3, c_re, c_im)
    mm = _dot_nt(x_re, y_re, HIGHEST) - _dot_nt(x_im, y_im, HIGHEST)
    ri = lax.broadcasted_iota(jnp.int32, (L * H, L * H), 0)
    ci = lax.broadcasted_iota(jnp.int32, (L * H, L * H), 1)
    mm = jnp.where((ci >> 4) >= (ri >> 4), mm, 0.0) + jnp.where(ri == ci, dsk_ref[0], 0.0)
    m_ref[0] = mm.astype(BF16)
    bzr_ref[0] = z_re.astype(BF16)
    bzi_ref[0] = z_im.astype(BF16)
    cyr_ref[0] = w_re.astype(BF16)
    cyi_ref[0] = (-w_im).astype(BF16)
    m16 = jnp.exp(lr * dt * L)
    ar_ref[0] = m16 * jnp.cos(li * dt * L)
    ai_ref[0] = m16 * jnp.sin(li * dt * L)


def _s5_prep(lam_re, lam_im, log_dt, b_re, b_im, c_re, c_im, d_skip):
    g, p, h = b_re.shape
    lh = S5_CHUNK * h
    g3 = lambda i: (i, 0, 0)
    ins = [lam_re.reshape(g, 1, p), lam_im.reshape(g, 1, p),
           jnp.broadcast_to(log_dt.reshape(g, 1, 1), (g, 1, p)),
           b_re.transpose(0, 2, 1), b_im.transpose(0, 2, 1), c_re, c_im,
           jnp.tile(d_skip, (1, S5_CHUNK)).reshape(g, 1, lh)]
    in_specs = [pl.BlockSpec((1,) + a.shape[1:], g3) for a in ins]
    out_shape = [jax.ShapeDtypeStruct((g, lh, lh), BF16)] + \
                [jax.ShapeDtypeStruct((g, lh, p), BF16)] * 4 + \
                [jax.ShapeDtypeStruct((g, 1, p), F32)] * 2
    out_specs = [pl.BlockSpec((1,) + s.shape[1:], g3) for s in out_shape]
    return pl.pallas_call(
        _s5prep_kernel, grid=(g,), in_specs=in_specs, out_specs=out_specs, out_shape=out_shape,
        compiler_params=_params(("arbitrary",)), name="s5_prep",
    )(*ins)


def _s5_kernel(u_ref, m_ref, bzr_ref, bzi_ref, cyr_ref, cyi_ref, ar_ref, ai_ref, y_ref,
               zr_sc, zi_sc, xr_sc, xi_sc, *, nch):
    L = S5_CHUNK
    xcat = jnp.concatenate([u_ref[k, 0] for k in range(L)], axis=1)
    zr_sc[...] = _dot(xcat, bzr_ref[0])
    zi_sc[...] = _dot(xcat, bzi_ref[0])
    a_r = ar_ref[0]
    a_i = ai_ref[0]

    def body(c, carry):
        x_r, x_i = carry
        xr_sc[pl.ds(c, 1), :] = x_r
        xi_sc[pl.ds(c, 1), :] = x_i
        z_r = zr_sc[pl.ds(c, 1), :]
        z_i = zi_sc[pl.ds(c, 1), :]
        return a_r * x_r - a_i * x_i + z_r, a_r * x_i + a_i * x_r + z_i

    zero = jnp.zeros((1, zr_sc.shape[1]), F32)
    lax.fori_loop(0, nch, body, (zero, zero))
    x_r = xr_sc[...].astype(BF16)
    x_i = xi_sc[...].astype(BF16)
    tw = 2 * LANES
    for n in range(L // 2):
        kk = tw * (n + 1)
        cols = slice(n * tw, (n + 1) * tw)
        y = (_dot(xcat[:, :kk], m_ref[0, :kk, cols]) + _dot(x_r, cyr_ref[0, :, cols])
             + _dot(x_i, cyi_ref[0, :, cols]))
        y_ref[2 * n, 0] = y[:, :LANES].astype(BF16)
        y_ref[2 * n + 1, 0] = y[:, LANES:].astype(BF16)


def _s5_block_operators(ops, gb):
    m, bzr, bzi, cyr, cyi, a_r, a_i = ops
    g, lh, p = bzr.shape
    L, H = S5_CHUNK, SSM_GROUP
    nb = g // gb
    eye = jnp.eye(gb, dtype=m.dtype)
    m2 = jnp.einsum('jgkatb,gc->jkgatcb', m.reshape(nb, gb, L, H, L, H), eye).reshape(nb, L * gb * H, L * gb * H)
    bz = lambda z: jnp.einsum('jgkap,gc->jkgacp', z.reshape(nb, gb, L, H, p), eye).reshape(nb, L * gb * H, gb * p)
    cy = lambda z: jnp.einsum('jgtbp,gc->jcptgb', z.reshape(nb, gb, L, H, p), eye).reshape(nb, gb * p, L * gb * H)
    return m2, bz(bzr), bz(bzi), cy(cyr), cy(cyi), a_r.reshape(nb, 1, gb * p), a_i.reshape(nb, 1, gb * p)


def _s5_scan(u_k, ops, bsz):
    L, nb, rows, _ = u_k.shape
    gb = LANES // SSM_GROUP
    m2, bzr, bzi, cyr, cyi, a_r, a_i = _s5_block_operators(ops, gb)
    nch = rows // bsz
    w = L * LANES
    sw = gb * SSM_STATE
    op3 = lambda j, b: (j, 0, 0)
    xspec = pl.BlockSpec((L, 1, nch, LANES), lambda j, b: (0, j, b, 0))
    in_specs = [xspec, pl.BlockSpec((1, w, w), op3),
                pl.BlockSpec((1, w, sw), op3), pl.BlockSpec((1, w, sw), op3),
                pl.BlockSpec((1, sw, w), op3), pl.BlockSpec((1, sw, w), op3),
                pl.BlockSpec((1, 1, sw), op3), pl.BlockSpec((1, 1, sw), op3)]
    return pl.pallas_call(
        functools.partial(_s5_kernel, nch=nch),
        grid=(nb, bsz),
        in_specs=in_specs,
        out_specs=xspec,
        out_shape=jax.ShapeDtypeStruct(u_k.shape, BF16),
        scratch_shapes=[pltpu.VMEM((nch, sw), F32)] * 4,
        compiler_params=_params(("arbitrary", "arbitrary")),
        name="s5_scan",
    )(u_k, m2, bzr, bzi, cyr, cyi, a_r, a_i)


def _cmp_kernel(tok_ref, w1x_ref, pe_ref, w1_ref, w2x_ref, o_ref, *, transpose_out):
    nh = N_KV_HEADS * CMP_HIDDEN
    g = _dot(tok_ref[...], w1x_ref[...])
    rows = g.shape[0]
    bias = jnp.dot(jnp.broadcast_to(pe_ref[...], (8, pe_ref.shape[1])), w1_ref[...],
                   preferred_element_type=F32, precision=HIGHEST)[0:1]
    bias = jnp.concatenate([bias] * N_KV_HEADS, axis=1)
    hid = g[:, :nh] + pltpu.roll(g[:, nh:], rows - 1, 0) + bias
    out = _dot(jax.nn.gelu(hid).astype(BF16), w2x_ref[...])
    if transpose_out:
        o_ref[0] = out.T.astype(BF16)
    else:
        for hh in range(N_KV_HEADS):
            o_ref[0, hh] = out[:, hh * HEAD_DIM:(hh + 1) * HEAD_DIM].astype(BF16)


def _compress(tok, pe, w1, w2, bsz, seq, transpose_out):
    ncp = seq // CMP_STRIDE
    half = CMP_STRIDE * HEAD_DIM
    eye = jnp.eye(N_KV_HEADS, dtype=F32)
    w1r = w1.reshape(2, CMP_STRIDE, HEAD_DIM, CMP_HIDDEN)
    w1x = jnp.einsum('zkdj,hg->khdzgj', w1r, eye).reshape(CMP_STRIDE * KV_DIM, 2 * N_KV_HEADS * CMP_HIDDEN)
    w2x = jnp.einsum('jd,hg->hjgd', w2, eye).reshape(N_KV_HEADS * CMP_HIDDEN, KV_DIM)
    tok2 = tok.reshape(bsz * ncp, CMP_STRIDE * KV_DIM)
    if transpose_out:
        out_shape = jax.ShapeDtypeStruct((bsz, KV_DIM, ncp), BF16)
        out_spec = pl.BlockSpec((1, KV_DIM, ncp), lambda b: (b, 0, 0))
    else:
        out_shape = jax.ShapeDtypeStruct((bsz, N_KV_HEADS, ncp, HEAD_DIM), BF16)
        out_spec = pl.BlockSpec((1, N_KV_HEADS, ncp, HEAD_DIM), lambda b: (b, 0, 0, 0))
    return pl.pallas_call(
        functools.partial(_cmp_kernel, transpose_out=transpose_out),
        grid=(bsz,),
        in_specs=[pl.BlockSpec((ncp, CMP_STRIDE * KV_DIM), lambda b: (b, 0)),
                  _const_spec(w1x.shape), _const_spec((1, 2 * half)), _const_spec(w1.shape),
                  _const_spec(w2x.shape)],
        out_specs=out_spec, out_shape=out_shape,
        compiler_params=_params(("arbitrary",)),
        name="compress_k" if transpose_out else "compress_v",
    )(tok2, w1x.astype(BF16), pe.reshape(1, 2 * half), w1, w2x.astype(BF16))


def _nsa_kernel(q_ref, kct_ref, vc_ref, kst_ref, vs_ref, kwt_ref, vw_ref, gt_ref, amat_ref, o_ref,
                rhs_sc, lhs_sc, m_sc, st_sc, mw_sc, stw_sc, *, tq, seq):
    nblk = seq // SEL_BLOCK
    ncp = seq // CMP_STRIDE
    rows = GQA * tq
    nslab = tq // LANES
    sub = min(ATT_SUB_ROWS, tq)
    qi = pl.program_id(2)
    t0 = qi * tq

    @pl.when(qi == 0)
    def _():
        rhs_sc[0:HEAD_DIM, :] = kst_ref[0]
        rhs_sc[HEAD_DIM:LANES, :] = jnp.zeros((LANES - HEAD_DIM, seq), BF16)
        cw = min(seq, 1024)

        def fill(c, _):
            col = c * cw + lax.broadcasted_iota(jnp.int32, (nblk, cw), 1)
            blk = lax.broadcasted_iota(jnp.int32, (nblk, cw), 0)
            onehot = jnp.where((col >> 6) == blk, 1.0, 0.0).astype(BF16)
            rhs_sc[LANES:LANES + nblk, pl.ds(pl.multiple_of(c * cw, cw), cw)] = onehot
            return 0

        lax.fori_loop(0, seq // cw, fill, 0)

    t_q = t0 + lax.broadcasted_iota(jnp.int32, (tq, 1), 0)

    def reset(state):
        state[0][...] = jnp.full((rows, LANES), NEG_BIG, F32)
        state[1][...] = jnp.zeros((rows, LANES), F32)

    def online(state, g, s, v_aug, h=None):
        m_ref, st_ref = state
        sl = slice(g * tq, (g + 1) * tq) if h is None else slice(g * tq + h * sub, g * tq + (h + 1) * sub)
        m_old = m_ref[sl, :]
        mx = s[:, 0:LANES]
        for j in range(1, nslab):
            mx = jnp.maximum(mx, s[:, j * LANES:(j + 1) * LANES])
        m_new = jnp.maximum(m_old, jnp.max(mx, axis=-1, keepdims=True))
        alpha = jnp.exp(m_old - m_new)
        p = jnp.concatenate([jnp.exp(s[:, j * LANES:(j + 1) * LANES] - m_new) for j in range(nslab)], axis=1)
        st_ref[sl, :] = alpha * st_ref[sl, :] + _dot(p.astype(BF16), v_aug)
        m_ref[sl, :] = m_new

    def result(state, g):
        st = state[1][g * tq:(g + 1) * tq, :]
        return st[:, 0:HEAD_DIM] * (1.0 / st[:, HEAD_DIM:HEAD_DIM + 1])

    def kpos(kj):
        return kj * tq + lax.broadcasted_iota(jnp.int32, (1, tq), 1)

    win_st = (mw_sc, stw_sc)
    reset(win_st)
    for w in range(WINDOW // tq + 1):
        kj = qi - WINDOW // tq + w
        off = pl.multiple_of(jnp.maximum(kj, 0) * tq, tq)
        kt = kwt_ref[0, :, pl.ds(off, tq)]
        v_aug = vw_ref[0, 0, pl.ds(off, tq), :]
        kp = kpos(kj)
        mask = (kp <= t_q) & (kp > t_q - WINDOW) & (kp >= 0)
        for g in range(GQA):
            online(win_st, g, jnp.where(mask, _dot(q_ref[0, g], kt), NEG_BIG), v_aug)

    n_col = lax.broadcasted_iota(jnp.int32, (1, ncp), 1)
    valid = (n_col * CMP_STRIDE + (CMP_BLOCK - 1)) <= t_q
    kct = kct_ref[0]
    vc = vc_ref[0, 0]
    o_cmp = []
    psum = None
    for g in range(GQA):
        sm = jnp.where(valid, _dot(q_ref[0, g], kct), NEG_BIG)
        e = jnp.exp(sm - jnp.max(sm, axis=-1, keepdims=True))
        p = jnp.where(valid, e * (1.0 / jnp.sum(e, axis=-1, keepdims=True)), 0.0)
        o_cmp.append(_dot(p.astype(BF16), vc))
        psum = p if g == 0 else psum + p

    amat = amat_ref[...]
    p_hi = psum.astype(BF16)
    r1 = psum - p_hi.astype(F32)
    p_mid = r1.astype(BF16)
    p_lo = (r1 - p_mid.astype(F32)).astype(BF16)
    imp = _dot(p_hi, amat) + _dot(p_mid, amat) + _dot(p_lo, amat)
    blk = lax.broadcasted_iota(jnp.int32, (nblk, 1), 0)
    t_l = t0 + lax.broadcasted_iota(jnp.int32, (1, tq), 1)
    cur = t_l >> 6
    forced = (blk == 0) | (blk == cur) | (blk == cur - 1)
    impm = jnp.where(forced, -NEG_BIG, jnp.where(blk * SEL_BLOCK <= t_l, imp.T, NEG_BIG))
    impm = impm.reshape(nblk // 8, 8, tq)
    blk_f = (lax.broadcasted_iota(jnp.int32, (nblk // 8, 8, tq), 0) * 8
             + lax.broadcasted_iota(jnp.int32, (nblk // 8, 8, tq), 1)).astype(F32)

    def over_blocks(x, op):
        r = x[0]
        for a in range(1, nblk // 8):
            r = op(r, x[a])
        for sh in (4, 2, 1):
            r = op(r, pltpu.roll(r, sh, 0))
        return r[None]

    selneg = jnp.full((nblk // 8, 8, tq), NEG_BIG, F32)
    for _ in range(min(N_SELECT, nblk)):
        best = over_blocks(impm, jnp.maximum)
        idx = over_blocks(jnp.where(impm == best, blk_f, float(nblk)), jnp.minimum)
        pick = blk_f == idx
        selneg = jnp.where(pick, 0.0, selneg)
        impm = jnp.where(pick, -jnp.inf, impm)
    selneg = selneg.reshape(nblk, tq).T.astype(BF16)

    for g in range(GQA):
        sl = slice(g * tq, (g + 1) * tq)
        lhs_sc[sl, 0:HEAD_DIM] = q_ref[0, g]
        lhs_sc[sl, HEAD_DIM:LANES] = jnp.zeros((tq, LANES - HEAD_DIM), BF16)
        lhs_sc[sl, LANES:LANES + nblk] = selneg

    sel_st = (m_sc, st_sc)
    reset(sel_st)

    def sel_scores(kj):
        rhs = rhs_sc[:, pl.ds(pl.multiple_of(kj * tq, tq), tq)]
        return tuple(_dot(lhs_sc[g * tq + h * sub:g * tq + (h + 1) * sub, :], rhs)
                     for g in range(GQA) for h in range(tq // sub))

    def sel_update(kj, scores, mask):
        v_aug = vs_ref[0, 0, pl.ds(pl.multiple_of(kj * tq, tq), tq), :]
        for g in range(GQA):
            for h in range(tq // sub):
                sc = scores[g * (tq // sub) + h]
                if mask is not None:
                    sc = jnp.where(mask[h * sub:(h + 1) * sub], sc, NEG_BIG)
                online(sel_st, g, sc, v_aug, h)

    def sel_step(kj, scores):
        nxt = sel_scores(kj + 1)
        sel_update(kj, scores, None)
        return nxt

    scores = lax.fori_loop(0, qi, sel_step, sel_scores(0))
    sel_update(qi, scores, kpos(qi) <= t_q)

    gt = gt_ref[0]
    for g in range(GQA):
        o = (gt[:, 3 * g:3 * g + 1] * o_cmp[g] + gt[:, 3 * g + 1:3 * g + 2] * result(sel_st, g)
             + gt[:, 3 * g + 2:3 * g + 3] * result(win_st, g))
        o_ref[:, g * HEAD_DIM:(g + 1) * HEAD_DIM] = o


def _importance_matrix(seq):
    ncp = seq // CMP_STRIDE
    nblk = seq // SEL_BLOCK
    ratio = SEL_BLOCK // CMP_STRIDE
    n = jnp.arange(ncp)[:, None]
    j = jnp.arange(nblk)[None, :]
    a = sum(((n == ratio * j + r).astype(F32) + (n == ratio * j + r + 1).astype(F32)) for r in range(ratio))
    return jnp.where(n < ncp - 1, a, 0.0).astype(BF16)


def _nsa_attention(q_hm, kct, vc_hm, kst, vs_hm, kwt, vw_hm, gates, bsz, seq):
    tq = min(ATT_TILE, seq)
    nq = seq // tq
    nblk = seq // SEL_BLOCK
    ncp = seq // CMP_STRIDE
    rows = GQA * tq
    amat = _importance_matrix(seq)
    head = lambda b, h, i: (b, h, 0, 0)
    headt = lambda b, h, i: (b, h, 0)
    in_specs = [pl.BlockSpec((1, GQA, tq, HEAD_DIM), lambda b, h, i: (b, h, i, 0)),
                pl.BlockSpec((1, HEAD_DIM, ncp), headt),
                pl.BlockSpec((1, 1, ncp, HEAD_DIM), head),
                pl.BlockSpec((1, HEAD_DIM, seq), headt),
                pl.BlockSpec((1, 1, seq, LANES), head),
                pl.BlockSpec((1, HEAD_DIM, seq), headt),
                pl.BlockSpec((1, 1, seq, LANES), head),
                pl.BlockSpec((1, tq, LANES), lambda b, h, i: (h, b * nq + i, 0)),
                _const_spec(amat.shape)]
    return pl.pallas_call(
        functools.partial(_nsa_kernel, tq=tq, seq=seq),
        grid=(bsz, N_KV_HEADS, nq),
        in_specs=in_specs,
        out_specs=pl.BlockSpec((tq, GQA * HEAD_DIM), lambda b, h, i: (b * nq + i, h)),
        out_shape=jax.ShapeDtypeStruct((bsz * seq, N_HEADS * HEAD_DIM), F32),
        scratch_shapes=[pltpu.VMEM((LANES + nblk, seq), BF16),
                        pltpu.VMEM((rows, LANES + nblk), BF16),
                        pltpu.VMEM((rows, LANES), F32), pltpu.VMEM((rows, LANES), F32),
                        pltpu.VMEM((rows, LANES), F32), pltpu.VMEM((rows, LANES), F32)],
        compiler_params=_params(("arbitrary", "arbitrary", "arbitrary")),
        name="nsa_attention",
    )(q_hm, kct, vc_hm, kst, vs_hm, kwt, vw_hm, gates, amat)


def _out_kernel(ys_ref, ya_ref, x_ref, wglu_ref, bglu_ref, gs_ref, ga_ref, woa_ref, wob_ref,
                ga1_ref, gf_ref, sh2_ref, sc2_ref, x1_ref, h2_ref, ysc):
    nchunk = ysc.shape[1] // S5_CHUNK
    for j in range(ysc.shape[0]):
        for k in range(S5_CHUNK):
            ysc[j, pl.ds(k, nchunk, stride=S5_CHUNK), :] = ys_ref[k, j].astype(F32)
    z = jax.nn.gelu(jnp.concatenate([ysc[j] for j in range(ysc.shape[0])], axis=1))
    glu = z * jax.nn.sigmoid(_dot(z.astype(BF16), wglu_ref[...]) + bglu_ref[...])
    n1 = _rms(glu, gs_ref[...]).astype(BF16)
    n2 = _rms(ya_ref[...], ga_ref[...]).astype(BF16)
    y = _dot(n1, woa_ref[...]) + _dot(n2, wob_ref[...])
    x1 = x_ref[...] + ga1_ref[0] * y
    x1_ref[...] = x1
    h2_ref[...] = (_rms(x1, gf_ref[...]) * (1.0 + sc2_ref[0]) + sh2_ref[0]).astype(BF16)


def _out_proj(y_ssm, y_att, x2, w_glu, b_glu, g_ssm, g_nsa, w_out, ga1, g_ffn, sh2, sc2, bsz, seq, tm):
    t, d = x2.shape
    d_att = y_att.shape[1]
    d_ssm = d - d_att
    nt = seq // tm
    row = lambda i: (i, 0)
    bat = lambda i: (i // nt, 0, 0)
    w_out = w_out.astype(BF16)
    in_specs = [pl.BlockSpec((S5_CHUNK, d_ssm // LANES, tm // S5_CHUNK, LANES), lambda i: (0, 0, i, 0)),
                pl.BlockSpec((tm, d_att), row), pl.BlockSpec((tm, d), row),
                _const_spec((d_ssm, d_ssm)), _const_spec((1, d_ssm)), _const_spec((1, d_ssm)),
                _const_spec((1, d_att)), _const_spec((d_ssm, d)), _const_spec((d_att, d)),
                pl.BlockSpec((1, 1, d), bat), _const_spec((1, d)),
                pl.BlockSpec((1, 1, d), bat), pl.BlockSpec((1, 1, d), bat)]
    return pl.pallas_call(
        _out_kernel,
        grid=(t // tm,),
        in_specs=in_specs,
        out_specs=[pl.BlockSpec((tm, d), row), pl.BlockSpec((tm, d), row)],
        out_shape=[jax.ShapeDtypeStruct((t, d), F32), jax.ShapeDtypeStruct((t, d), BF16)],
        scratch_shapes=[pltpu.VMEM((d_ssm // LANES, tm, LANES), F32)],
        compiler_params=_params(("arbitrary",)),
        name="out_proj",
    )(y_ssm, y_att, x2, w_glu.astype(BF16), b_glu.reshape(1, d_ssm), g_ssm.reshape(1, d_ssm),
      g_nsa.reshape(1, d_att), w_out[:d_ssm], w_out[d_ssm:], ga1.reshape(bsz, 1, d), g_ffn.reshape(1, d),
      sh2.reshape(bsz, 1, d), sc2.reshape(bsz, 1, d))


FFN_HALO = 16


def _ffn_kernel(h_ref, halo_ref, x1_ref, wv_ref, wg_ref, cwv_ref, cwg_ref, cbv_ref, cbg_ref, wd_ref,
                ga2_ref, gfin_ref, o_ref, act_sc, acc_sc, *, nt, nj, nsteps):
    s = pl.program_id(0)
    su = jnp.minimum(s, nsteps - 1)
    sd = jnp.maximum(s - 1, 0)
    iu = lax.div(su, nj)
    jd = lax.rem(sd, nj)
    slot = lax.rem(s, 2)

    @pl.when(s == 0)
    def _():
        act_sc[...] = jnp.zeros(act_sc.shape, BF16)
        acc_sc[...] = jnp.zeros(acc_sc.shape, F32)

    part = _dot(act_sc[1 - slot], wd_ref[...])
    acc_sc[...] = jnp.where(jd > 0, acc_sc[...], 0.0) + part

    halo = halo_ref[...]
    halo = jnp.where(lax.rem(iu, nt) == 0, jnp.zeros_like(halo), halo)
    hx = jnp.concatenate([halo, h_ref[...]], axis=0)

    def conv(w_ref, cw_ref, cb_ref):
        up = _dot(hx, w_ref[...])
        cw = cw_ref[...]
        c = (pltpu.roll(up, 2, 0) * cw[0:1] + pltpu.roll(up, 1, 0) * cw[1:2] + up * cw[2:3])
        return c[FFN_HALO:] + cb_ref[...]

    val = conv(wv_ref, cwv_ref, cbv_ref)
    gate = conv(wg_ref, cwg_ref, cbg_ref)
    act_sc[slot] = (gate * jax.nn.sigmoid(gate) * val).astype(BF16)

    @pl.when((s > 0) & (jd == nj - 1))
    def _():
        x2 = x1_ref[...] + ga2_ref[0] * acc_sc[...]
        o_ref[...] = _rms(x2, gfin_ref[...])


def _conv_ffn(h2, x1, w_up, conv_w, conv_b, w_down, ga2, g_final, bsz, seq, tm, tn):
    t, d = x1.shape
    dff = w_down.shape[0]
    nt = seq // tm
    nj = dff // tn
    nsteps = (t // tm) * nj
    w_up = w_up.astype(BF16)
    w_down = w_down.astype(BF16)
    cb = conv_b.reshape(1, 2 * dff)
    hb = tm // FFN_HALO
    iu = lambda s: lax.div(jnp.minimum(s, nsteps - 1), nj)
    ju = lambda s: lax.rem(jnp.minimum(s, nsteps - 1), nj)
    idn = lambda s: lax.div(jnp.maximum(s - 1, 0), nj)
    jdn = lambda s: lax.rem(jnp.maximum(s - 1, 0), nj)
    in_specs = [pl.BlockSpec((tm, d), lambda s: (iu(s), 0)),
                pl.BlockSpec((FFN_HALO, d), lambda s: (jnp.maximum(iu(s) * hb - 1, 0), 0)),
                pl.BlockSpec((tm, d), lambda s: (idn(s), 0)),
                pl.BlockSpec((d, tn), lambda s: (0, ju(s))),
                pl.BlockSpec((d, tn), lambda s: (0, nj + ju(s))),
                pl.BlockSpec((CONV_WIDTH, tn), lambda s: (0, ju(s))),
                pl.BlockSpec((CONV_WIDTH, tn), lambda s: (0, nj + ju(s))),
                pl.BlockSpec((1, tn), lambda s: (0, ju(s))),
                pl.BlockSpec((1, tn), lambda s: (0, nj + ju(s))),
                pl.BlockSpec((tn, d), lambda s: (jdn(s), 0)),
                pl.BlockSpec((1, 1, d), lambda s: (lax.div(idn(s), nt), 0, 0)),
                pl.BlockSpec((1, d), lambda s: (0, 0))]
    return pl.pallas_call(
        functools.partial(_ffn_kernel, nt=nt, nj=nj, nsteps=nsteps),
        grid=(nsteps + 1,),
        in_specs=in_specs,
        out_specs=pl.BlockSpec((tm, d), lambda s: (idn(s), 0)),
        out_shape=jax.ShapeDtypeStruct((t, d), F32),
        scratch_shapes=[pltpu.VMEM((2, tm, tn), BF16), pltpu.VMEM((tm, d), F32)],
        compiler_params=_params(("arbitrary",)),
        name="conv_ffn",
    )(h2, h2, x1, w_up, w_up, conv_w, conv_w, cb, cb, w_down, ga2.reshape(bsz, 1, d), g_final.reshape(1, d))


def _block(x, mod, g_mix_norm, w_in, s5_params, cmp_k, cmp_v, g_ssm_out, g_nsa_out, w_glu, b_glu, w_out,
           g_ffn_norm, w_up, conv_w, conv_b, w_down, g_final):
    bsz, seq, d = x.shape
    t = bsz * seq
    d_ssm = d // 2
    x2 = x.reshape(t, d)
    sh1, sc1, ga1, sh2, sc2, ga2 = jnp.split(mod, 6, axis=-1)
    tm = min(512, seq)

    u, q_hm, kc_tok, vc_tok, kst, vs_hm, kwt, vw_hm, gates = _input_proj(
        x2, g_mix_norm, sh1, sc1, w_in, _rope_tables(seq), bsz, seq, tm)

    y_ssm = _s5_scan(u, _s5_prep(*s5_params), bsz)

    kct = _compress(kc_tok, *cmp_k, bsz, seq, True)
    vc_hm = _compress(vc_tok, *cmp_v, bsz, seq, False)
    y_att = _nsa_attention(q_hm, kct, vc_hm, kst, vs_hm, kwt, vw_hm, gates, bsz, seq)

    x1, h2 = _out_proj(y_ssm, y_att, x2, w_glu, b_glu, g_ssm_out, g_nsa_out, w_out, ga1, g_ffn_norm,
                       sh2, sc2, bsz, seq, tm)
    out = _conv_ffn(h2, x1, w_up, conv_w, conv_b, w_down, ga2, g_final, bsz, seq, tm, 512)
    return out.reshape(bsz, seq, d)


def kernel(x, c, w_ada, b_ada, g_mix_norm, w_in, lam_re, lam_im, log_dt, b_re, b_im, c_re, c_im, d_skip,
           w_glu, b_glu, pe_k, w1_k, w2_k, pe_v, w1_v, w2_v, g_ssm_out, g_nsa_out, w_out, g_ffn_norm,
           w_up, conv_w, conv_b, w_down, g_final):
    l = 0
    mod = _adaln_mod(c, w_ada[l], b_ada[l])
    return _block(x, mod, g_mix_norm[l], w_in[l],
                  (lam_re[l], lam_im[l], log_dt[l], b_re[l], b_im[l], c_re[l], c_im[l], d_skip[l]),
                  (pe_k[l], w1_k[l], w2_k[l]), (pe_v[l], w1_v[l], w2_v[l]),
                  g_ssm_out[l], g_nsa_out[l], w_glu[l], b_glu[l], w_out[l], g_ffn_norm[l],
                  w_up[l], conv_w[l], conv_b[l], w_down[l], g_final)
```

```python
import functools
import math

import jax
import jax.numpy as jnp
from jax import lax
from jax.experimental import pallas as pl
from jax.experimental.pallas import tpu as pltpu

F32 = jnp.float32
BF16 = jnp.bfloat16
HIGHEST = lax.Precision.HIGHEST

SSM_GROUP = 16
SSM_STATE = 64
N_HEADS = 16
N_KV_HEADS = 4
HEAD_DIM = 64
GQA = N_HEADS // N_KV_HEADS
KV_DIM = N_KV_HEADS * HEAD_DIM
ROT_DIM = HEAD_DIM // 4
ROPE_THETA = 500000.0
CMP_BLOCK = 32
CMP_STRIDE = 16
CMP_HIDDEN = 2 * HEAD_DIM
SEL_BLOCK = 64
N_SELECT = 16
WINDOW = 512
CONV_WIDTH = 3
NORM_EPS = 1e-6
NEG_BIG = -1e30

LANES = 128
S5_CHUNK = 16
ATT_TILE = 256
ATT_SUB_ROWS = 256
VMEM_LIMIT = 56 * 2**20


def _params(sem, vmem=VMEM_LIMIT):
    return pltpu.CompilerParams(dimension_semantics=sem, vmem_limit_bytes=vmem)


def _const_spec(shape):
    n = len(shape)
    return pl.BlockSpec(shape, lambda *_: (0,) * n)


def _rms(x, g):
    return x * lax.rsqrt(jnp.mean(x * x, axis=-1, keepdims=True) + NORM_EPS) * g


def _dot(a, b):
    return jnp.dot(a, b, preferred_element_type=F32)


def _dot_nt(a, b, precision=None):
    return lax.dot_general(a, b, (((1,), (1,)), ((), ())), precision=precision,
                           preferred_element_type=F32)


def _mod_kernel(c_ref, w_ref, b_ref, o_ref):
    c = c_ref[...]
    sc = c * jax.nn.sigmoid(c)
    o_ref[...] = jnp.dot(sc, w_ref[...], preferred_element_type=F32, precision=HIGHEST) + b_ref[...]


def _adaln_mod(c, w_ada, b_ada):
    bsz, d = c.shape
    n = w_ada.shape[1]
    tn = n // 8
    cp = jnp.zeros((8, d), F32).at[:bsz].set(c)
    out = pl.pallas_call(
        _mod_kernel,
        grid=(n // tn,),
        in_specs=[pl.BlockSpec((8, d), lambda j: (0, 0)),
                  pl.BlockSpec((d, tn), lambda j: (0, j)),
                  pl.BlockSpec((1, tn), lambda j: (0, j))],
        out_specs=pl.BlockSpec((8, tn), lambda j: (0, j)),
        out_shape=jax.ShapeDtypeStruct((8, n), F32),
        compiler_params=_params(("arbitrary",)),
        name="adaln_mod",
    )(cp, w_ada, b_ada.reshape(1, n))
    return out[:bsz]


def _rope_kernel(c_ref, sm_ref, sp_ref, *, tr):
    i = pl.program_id(0)
    pos = (i * tr + lax.broadcasted_iota(jnp.int32, (tr, LANES), 0)).astype(F32)
    lane = lax.broadcasted_iota(jnp.int32, (tr, LANES), 1)
    d = lane & (HEAD_DIM - 1)
    half = ROT_DIM // 2
    fi = d & (half - 1)
    inv = jnp.zeros((tr, LANES), F32)
    for k in range(half):
        inv = jnp.where(fi == k, ROPE_THETA ** (-k / half), inv)
    ang = pos * inv
    cs = jnp.cos(ang)
    sn = jnp.sin(ang)
    c_ref[...] = jnp.where(d < ROT_DIM, cs, 1.0)
    sm_ref[...] = jnp.where(d < half, -sn, 0.0)
    sp_ref[...] = jnp.where((d >= half) & (d < ROT_DIM), sn, 0.0)


def _rope_tables(seq):
    tr = min(seq, 1024)
    sds = jax.ShapeDtypeStruct((seq, LANES), F32)
    spec = pl.BlockSpec((tr, LANES), lambda i: (i, 0))
    return pl.pallas_call(
        functools.partial(_rope_kernel, tr=tr),
        grid=(seq // tr,),
        in_specs=[],
        out_specs=[spec, spec, spec],
        out_shape=[sds, sds, sds],
        compiler_params=_params(("arbitrary",)),
        name="rope_tables",
    )()


def _proj_kernel(x_ref, g_ref, sh_ref, sc_ref, wu_ref, wq_ref, wkv_ref, wg_ref, rc_ref, rm_ref, rp_ref,
                 u_ref, q_ref, kc_ref, vc_ref, kst_ref, vs_ref, kwt_ref, vw_ref, gt_ref, usc):
    x = x_ref[...]
    h = _rms(x, g_ref[...]) * (1.0 + sc_ref[0]) + sh_ref[0]
    hb = h.astype(BF16)
    rc, rm, rp = rc_ref[...], rm_ref[...], rp_ref[...]

    def rope(s):
        return s * rc + pltpu.roll(s, LANES - ROT_DIM // 2, 1) * rm + pltpu.roll(s, ROT_DIM // 2, 1) * rp

    def roped(a):
        return jnp.concatenate([rope(a[:, :LANES]), rope(a[:, LANES:])], axis=1)

    u = _dot(hb, wu_ref[...])
    nchunk = x.shape[0] // S5_CHUNK
    for j in range(u_ref.shape[1]):
        usc[j] = u[:, j * LANES:(j + 1) * LANES]
        for k in range(S5_CHUNK):
            u_ref[k, j] = usc[j, pl.ds(k, nchunk, stride=S5_CHUNK), :].astype(BF16)

    q = _dot(hb, wq_ref[...])
    scale = HEAD_DIM ** -0.5
    for j in range(N_HEADS // 2):
        s = rope(q[:, j * LANES:(j + 1) * LANES]) * scale
        q_ref[0, 2 * j] = s[:, :HEAD_DIM].astype(BF16)
        q_ref[0, 2 * j + 1] = s[:, HEAD_DIM:].astype(BF16)

    kv = _dot(hb, wkv_ref[...])

    def seg(i):
        return kv[:, i * KV_DIM:(i + 1) * KV_DIM]

    kc_ref[...] = roped(seg(0)).astype(BF16)
    vc_ref[...] = seg(1).astype(BF16)
    kst_ref[0] = roped(seg(2)).T.astype(BF16)
    vs = seg(3)
    kwt_ref[0] = roped(seg(4)).T.astype(BF16)
    vw = seg(5)
    ones_col = jnp.where(lax.broadcasted_iota(jnp.int32, (x.shape[0], LANES - HEAD_DIM), 1) == 0, 1.0, 0.0)
    for hh in range(N_KV_HEADS):
        sl = slice(hh * HEAD_DIM, (hh + 1) * HEAD_DIM)
        vs_ref[0, hh] = jnp.concatenate([vs[:, sl], ones_col], axis=1).astype(BF16)
        vw_ref[0, hh] = jnp.concatenate([vw[:, sl], ones_col], axis=1).astype(BF16)

    sg = jax.nn.sigmoid(_dot(hb, wg_ref[...]))
    gt_ref[0] = sg
    for hh in range(1, N_KV_HEADS):
        gt_ref[hh] = pltpu.roll(sg, LANES - 3 * GQA * hh, 1)


def _input_proj(x2, g, sh, sc, w_in, rope_tabs, bsz, seq, tm):
    t, d = x2.shape
    d_ssm = d // 2
    d_att = d - d_ssm
    nt = seq // tm
    w_in = w_in.astype(BF16)
    o = 0
    wu = w_in[:, o:o + d_ssm]; o += d_ssm
    wq = w_in[:, o:o + d_att]; o += d_att
    wkv = w_in[:, o:o + 6 * KV_DIM]; o += 6 * KV_DIM
    wg = jnp.zeros((d, LANES), BF16).at[:, :3 * N_HEADS].set(w_in[:, o:])
    row = lambda i: (i, 0)
    bat = lambda i: (i // nt, 0, 0)
    tab = lambda i: (i % nt, 0)
    hm = lambda i: (i // nt, 0, i % nt, 0)
    tr = lambda i: (i // nt, 0, i % nt)
    in_specs = [pl.BlockSpec((tm, d), row), _const_spec((1, d)),
                pl.BlockSpec((1, 1, d), bat), pl.BlockSpec((1, 1, d), bat),
                _const_spec(wu.shape), _const_spec(wq.shape), _const_spec(wkv.shape), _const_spec(wg.shape),
                pl.BlockSpec((tm, LANES), tab), pl.BlockSpec((tm, LANES), tab), pl.BlockSpec((tm, LANES), tab)]
    out_shape = [jax.ShapeDtypeStruct((S5_CHUNK, d_ssm // LANES, t // S5_CHUNK, LANES), BF16),
                 jax.ShapeDtypeStruct((bsz, N_HEADS, seq, HEAD_DIM), BF16),
                 jax.ShapeDtypeStruct((t, KV_DIM), BF16),
                 jax.ShapeDtypeStruct((t, KV_DIM), BF16),
                 jax.ShapeDtypeStruct((bsz, KV_DIM, seq), BF16),
                 jax.ShapeDtypeStruct((bsz, N_KV_HEADS, seq, LANES), BF16),
                 jax.ShapeDtypeStruct((bsz, KV_DIM, seq), BF16),
                 jax.ShapeDtypeStruct((bsz, N_KV_HEADS, seq, LANES), BF16),
                 jax.ShapeDtypeStruct((N_KV_HEADS, t, LANES), F32)]
    out_specs = [pl.BlockSpec((S5_CHUNK, d_ssm // LANES, tm // S5_CHUNK, LANES), lambda i: (0, 0, i, 0)),
                 pl.BlockSpec((1, N_HEADS, tm, HEAD_DIM), hm),
                 pl.BlockSpec((tm, KV_DIM), row),
                 pl.BlockSpec((tm, KV_DIM), row),
                 pl.BlockSpec((1, KV_DIM, tm), tr),
                 pl.BlockSpec((1, N_KV_HEADS, tm, LANES), hm),
                 pl.BlockSpec((1, KV_DIM, tm), tr),
                 pl.BlockSpec((1, N_KV_HEADS, tm, LANES), hm),
                 pl.BlockSpec((N_KV_HEADS, tm, LANES), lambda i: (0, i, 0))]
    return pl.pallas_call(
        _proj_kernel,
        grid=(t // tm,),
        in_specs=in_specs, out_specs=out_specs, out_shape=out_shape,
        scratch_shapes=[pltpu.VMEM((d_ssm // LANES, tm, LANES), F32)],
        compiler_params=_params(("arbitrary",)),
        name="input_proj",
    )(x2, g.reshape(1, d), sh.reshape(bsz, 1, d), sc.reshape(bsz, 1, d), wu, wq, wkv, wg, *rope_tabs)


def _s5prep_kernel(lr_ref, li_ref, ldt_ref, btr_ref, bti_ref, cr_ref, ci_ref, dsk_ref,
                   m_ref, bzr_ref, bzi_ref, cyr_ref, cyi_ref, ar_ref, ai_ref):
    L, H, P = S5_CHUNK, SSM_GROUP, SSM_STATE
    lr = lr_ref[0]
    li = li_ref[0]
    dt = jnp.exp(ldt_ref[0])
    mag = jnp.exp(lr * dt)
    ab_re = mag * jnp.cos(li * dt)
    ab_im = mag * jnp.sin(li * dt)
    den = lr * lr + li * li
    num_re = ab_re - 1.0
    f_re = (num_re * lr + ab_im * li) / den
    f_im = (ab_im * lr - num_re * li) / den
    btr = btr_ref[0]
    bti = bti_ref[0]
    bb_re = f_re * btr - f_im * bti
    bb_im = f_re * bti + f_im * btr
    c_re = cr_ref[0]
    c_im = ci_ref[0]

    r = lax.broadcasted_iota(jnp.int32, (4 * L, 1), 0)
    k = r & (L - 1)
    blk = r >> 4
    e = jnp.where(blk == 0, -k, jnp.where(blk == 1, k, jnp.where(blk == 2, L - 1 - k, k + 1))).astype(F32)
    pm = jnp.exp(lr * dt * e)
    ph = li * dt * e
    pr = pm * jnp.cos(ph)
    pi = pm * jnp.sin(ph)

    def rows(i, sm_r, sm_i):
        a_r = jnp.broadcast_to(pr[i * L:(i + 1) * L][:, None, :], (L, H, P)).reshape(L * H, P)
        a_i = jnp.broadcast_to(pi[i * L:(i + 1) * L][:, None, :], (L, H, P)).reshape(L * H, P)
        s_r = jnp.broadcast_to(sm_r[None], (L, H, P)).reshape(L * H, P)
        s_i = jnp.broadcast_to(sm_i[None], (L, H, P)).reshape(L * H, P)
        return a_r * s_r - a_i * s_i, a_r * s_i + a_i * s_r

    x_re, x_im = rows(0, bb_re, bb_im)
    y_re, y_im = rows(1, c_re, c_im)
    z_re, z_im = rows(2, bb_re, bb_im)
    w_re, w_im = rows(3, c_re, c_im)
    mm = _dot_nt(x_re, y_re, HIGHEST) - _dot_nt(x_im, y_im, HIGHEST)
    ri = lax.broadcasted_iota(jnp.int32, (L * H, L * H), 0)
    ci = lax.broadcasted_iota(jnp.int32, (L * H, L * H), 1)
    mm = jnp.where((ci >> 4) >= (ri >> 4), mm, 0.0) + jnp.where(ri == ci, dsk_ref[0], 0.0)
    m_ref[0] = mm.astype(BF16)
    bzr_ref[0] = z_re.astype(BF16)
    bzi_ref[0] = z_im.astype(BF16)
    cyr_ref[0] = w_re.astype(BF16)
    cyi_ref[0] = (-w_im).astype(BF16)
    m16 = jnp.exp(lr * dt * L)
    ar_ref[0] = m16 * jnp.cos(li * dt * L)
    ai_ref[0] = m16 * jnp.sin(li * dt * L)


def _s5_prep(lam_re, lam_im, log_dt, b_re, b_im, c_re, c_im, d_skip):
    g, p, h = b_re.shape
    lh = S5_CHUNK * h
    g3 = lambda i: (i, 0, 0)
    ins = [lam_re.reshape(g, 1, p), lam_im.reshape(g, 1, p),
           jnp.broadcast_to(log_dt.reshape(g, 1, 1), (g, 1, p)),
           b_re.transpose(0, 2, 1), b_im.transpose(0, 2, 1), c_re, c_im,
           jnp.tile(d_skip, (1, S5_CHUNK)).reshape(g, 1, lh)]
    in_specs = [pl.BlockSpec((1,) + a.shape[1:], g3) for a in ins]
    out_shape = [jax.ShapeDtypeStruct((g, lh, lh), BF16)] + \
                [jax.ShapeDtypeStruct((g, lh, p), BF16)] * 4 + \
                [jax.ShapeDtypeStruct((g, 1, p), F32)] * 2
    out_specs = [pl.BlockSpec((1,) + s.shape[1:], g3) for s in out_shape]
    return pl.pallas_call(
        _s5prep_kernel, grid=(g,), in_specs=in_specs, out_specs=out_specs, out_shape=out_shape,
        compiler_params=_params(("arbitrary",)), name="s5_prep",
    )(*ins)


def _s5_kernel(u_ref, m_ref, bzr_ref, bzi_ref, cyr_ref, cyi_ref, ar_ref, ai_ref, y_ref,
               zr_sc, zi_sc, xr_sc, xi_sc, *, nch):
    L = S5_CHUNK
    xcat = jnp.concatenate([u_ref[k, 0] for k in range(L)], axis=1)
    zr_sc[...] = _dot(xcat, bzr_ref[0])
    zi_sc[...] = _dot(xcat, bzi_ref[0])
    a_r = ar_ref[0]
    a_i = ai_ref[0]

    def body(c, carry):
        x_r, x_i = carry
        xr_sc[pl.ds(c, 1), :] = x_r
        xi_sc[pl.ds(c, 1), :] = x_i
        z_r = zr_sc[pl.ds(c, 1), :]
        z_i = zi_sc[pl.ds(c, 1), :]
        return a_r * x_r - a_i * x_i + z_r, a_r * x_i + a_i * x_r + z_i

    zero = jnp.zeros((1, zr_sc.shape[1]), F32)
    lax.fori_loop(0, nch, body, (zero, zero))
    x_r = xr_sc[...].astype(BF16)
    x_i = xi_sc[...].astype(BF16)
    tw = 2 * LANES
    for n in range(L // 2):
        kk = tw * (n + 1)
        cols = slice(n * tw, (n + 1) * tw)
        y = (_dot(xcat[:, :kk], m_ref[0, :kk, cols]) + _dot(x_r, cyr_ref[0, :, cols])
             + _dot(x_i, cyi_ref[0, :, cols]))
        y_ref[2 * n, 0] = y[:, :LANES].astype(BF16)
        y_ref[2 * n + 1, 0] = y[:, LANES:].astype(BF16)


def _s5_block_kernel(m_ref, bzr_ref, bzi_ref, cyr_ref, cyi_ref, m2_ref, bzr2_ref, bzi2_ref, cyr2_ref, cyi2_ref):
    L, H, P = S5_CHUNK, SSM_GROUP, SSM_STATE
    gb = LANES // H
    w = L * LANES
    r = lax.broadcasted_iota(jnp.int32, (L * H, w), 0)
    q = lax.broadcasted_iota(jnp.int32, (L * H, w), 1)
    same_tc = ((r >> 4) == (q >> 7)) & ((r & (H - 1)) == (q & (H - 1)))
    q_grp = (q >> 4) & (gb - 1)
    sr = lax.broadcasted_iota(jnp.int32, (P, gb * P), 0)
    sq = lax.broadcasted_iota(jnp.int32, (P, gb * P), 1)
    eye_p = jnp.where(lax.broadcasted_iota(jnp.int32, (P, P), 0) == lax.broadcasted_iota(jnp.int32, (P, P), 1),
                      1.0, 0.0).astype(BF16)
    for g in range(gb):
        spread_c = jnp.where(same_tc & (q_grp == g), 1.0, 0.0).astype(BF16)
        spread_s = jnp.where(sq - g * P == sr, 1.0, 0.0).astype(BF16)
        xm = _dot(m_ref[g], spread_c).astype(BF16)
        xr = _dot(bzr_ref[g], spread_s).astype(BF16)
        xi = _dot(bzi_ref[g], spread_s).astype(BF16)
        for k in range(L):
            rows = slice(k * LANES + g * H, k * LANES + (g + 1) * H)
            m2_ref[0, rows, :] = xm[k * H:(k + 1) * H, :]
            bzr2_ref[0, rows, :] = xr[k * H:(k + 1) * H, :]
            bzi2_ref[0, rows, :] = xi[k * H:(k + 1) * H, :]
        for src, dst in ((cyr_ref, cyr2_ref), (cyi_ref, cyi2_ref)):
            cy_t = _dot_nt(eye_p, src[g]).astype(BF16)
            dst[0, g * P:(g + 1) * P, :] = _dot(cy_t, spread_c).astype(BF16)


def _s5_block_operators(ops, gb):
    m, bzr, bzi, cyr, cyi, a_r, a_i = ops
    g, lh, p = bzr.shape
    nb = g // gb
    w = S5_CHUNK * LANES
    sw = gb * p
    g3 = lambda j: (j, 0, 0)
    out_shape = [jax.ShapeDtypeStruct((nb, w, w), BF16), jax.ShapeDtypeStruct((nb, w, sw), BF16),
                 jax.ShapeDtypeStruct((nb, w, sw), BF16), jax.ShapeDtypeStruct((nb, sw, w), BF16),
                 jax.ShapeDtypeStruct((nb, sw, w), BF16)]
    m2, bzr2, bzi2, cyr2, cyi2 = pl.pallas_call(
        _s5_block_kernel,
        grid=(nb,),
        in_specs=[pl.BlockSpec((gb, lh, lh), g3)] + [pl.BlockSpec((gb, lh, p), g3)] * 4,
        out_specs=[pl.BlockSpec((1,) + s.shape[1:], g3) for s in out_shape],
        out_shape=out_shape,
        compiler_params=_params(("arbitrary",)),
        name="s5_block_ops",
    )(m, bzr, bzi, cyr, cyi)
    return m2, bzr2, bzi2, cyr2, cyi2, a_r.reshape(nb, 1, sw), a_i.reshape(nb, 1, sw)


def _s5_scan(u_k, ops, bsz):
    L, nb, rows, _ = u_k.shape
    gb = LANES // SSM_GROUP
    m2, bzr, bzi, cyr, cyi, a_r, a_i = _s5_block_operators(ops, gb)
    nch = rows // bsz
    w = L * LANES
    sw = gb * SSM_STATE
    op3 = lambda j, b: (j, 0, 0)
    xspec = pl.BlockSpec((L, 1, nch, LANES), lambda j, b: (0, j, b, 0))
    in_specs = [xspec, pl.BlockSpec((1, w, w), op3),
                pl.BlockSpec((1, w, sw), op3), pl.BlockSpec((1, w, sw), op3),
                pl.BlockSpec((1, sw, w), op3), pl.BlockSpec((1, sw, w), op3),
                pl.BlockSpec((1, 1, sw), op3), pl.BlockSpec((1, 1, sw), op3)]
    return pl.pallas_call(
        functools.partial(_s5_kernel, nch=nch),
        grid=(nb, bsz),
        in_specs=in_specs,
        out_specs=xspec,
        out_shape=jax.ShapeDtypeStruct(u_k.shape, BF16),
        scratch_shapes=[pltpu.VMEM((nch, sw), F32)] * 4,
        compiler_params=_params(("arbitrary", "arbitrary")),
        name="s5_scan",
    )(u_k, m2, bzr, bzi, cyr, cyi, a_r, a_i)


def _cmp_kernel(tok_ref, w1x_ref, pe_ref, w1_ref, w2x_ref, o_ref, *, transpose_out):
    nh = N_KV_HEADS * CMP_HIDDEN
    g = _dot(tok_ref[...], w1x_ref[...])
    rows = g.shape[0]
    bias = jnp.dot(jnp.broadcast_to(pe_ref[...], (8, pe_ref.shape[1])), w1_ref[...],
                   preferred_element_type=F32, precision=HIGHEST)[0:1]
    bias = jnp.concatenate([bias] * N_KV_HEADS, axis=1)
    hid = g[:, :nh] + pltpu.roll(g[:, nh:], rows - 1, 0) + bias
    out = _dot(jax.nn.gelu(hid).astype(BF16), w2x_ref[...])
    if transpose_out:
        o_ref[0] = out.T.astype(BF16)
    else:
        for hh in range(N_KV_HEADS):
            o_ref[0, hh] = out[:, hh * HEAD_DIM:(hh + 1) * HEAD_DIM].astype(BF16)


def _compress(tok, pe, w1, w2, bsz, seq, transpose_out):
    ncp = seq // CMP_STRIDE
    half = CMP_STRIDE * HEAD_DIM
    eye = jnp.eye(N_KV_HEADS, dtype=F32)
    w1r = w1.reshape(2, CMP_STRIDE, HEAD_DIM, CMP_HIDDEN)
    w1x = jnp.einsum('zkdj,hg->khdzgj', w1r, eye).reshape(CMP_STRIDE * KV_DIM, 2 * N_KV_HEADS * CMP_HIDDEN)
    w2x = jnp.einsum('jd,hg->hjgd', w2, eye).reshape(N_KV_HEADS * CMP_HIDDEN, KV_DIM)
    tok2 = tok.reshape(bsz * ncp, CMP_STRIDE * KV_DIM)
    if transpose_out:
        out_shape = jax.ShapeDtypeStruct((bsz, KV_DIM, ncp), BF16)
        out_spec = pl.BlockSpec((1, KV_DIM, ncp), lambda b: (b, 0, 0))
    else:
        out_shape = jax.ShapeDtypeStruct((bsz, N_KV_HEADS, ncp, HEAD_DIM), BF16)
        out_spec = pl.BlockSpec((1, N_KV_HEADS, ncp, HEAD_DIM), lambda b: (b, 0, 0, 0))
    return pl.pallas_call(
        functools.partial(_cmp_kernel, transpose_out=transpose_out),
        grid=(bsz,),
        in_specs=[pl.BlockSpec((ncp, CMP_STRIDE * KV_DIM), lambda b: (b, 0)),
                  _const_spec(w1x.shape), _const_spec((1, 2 * half)), _const_spec(w1.shape),
                  _const_spec(w2x.shape)],
        out_specs=out_spec, out_shape=out_shape,
        compiler_params=_params(("arbitrary",)),
        name="compress_k" if transpose_out else "compress_v",
    )(tok2, w1x.astype(BF16), pe.reshape(1, 2 * half), w1, w2x.astype(BF16))


def _nsa_kernel(q_ref, kct_ref, vc_ref, kst_ref, vs_ref, kwt_ref, vw_ref, gt_ref, amat_ref, o_ref,
                rhs_sc, lhs_sc, m_sc, st_sc, mw_sc, stw_sc, *, tq, seq):
    nblk = seq // SEL_BLOCK
    ncp = seq // CMP_STRIDE
    rows = GQA * tq
    nslab = tq // LANES
    sub = min(ATT_SUB_ROWS, tq)
    qi = pl.program_id(2)
    t0 = qi * tq

    @pl.when(qi == 0)
    def _():
        rhs_sc[0:HEAD_DIM, :] = kst_ref[0]
        rhs_sc[HEAD_DIM:LANES, :] = jnp.zeros((LANES - HEAD_DIM, seq), BF16)
        cw = min(seq, 1024)

        def fill(c, _):
            col = c * cw + lax.broadcasted_iota(jnp.int32, (nblk, cw), 1)
            blk = lax.broadcasted_iota(jnp.int32, (nblk, cw), 0)
            onehot = jnp.where((col >> 6) == blk, 1.0, 0.0).astype(BF16)
            rhs_sc[LANES:LANES + nblk, pl.ds(pl.multiple_of(c * cw, cw), cw)] = onehot
            return 0

        lax.fori_loop(0, seq // cw, fill, 0)

    t_q = t0 + lax.broadcasted_iota(jnp.int32, (tq, 1), 0)

    def reset(state):
        state[0][...] = jnp.full((rows, LANES), NEG_BIG, F32)
        state[1][...] = jnp.zeros((rows, LANES), F32)

    def online(state, g, s, v_aug, h=None):
        m_ref, st_ref = state
        sl = slice(g * tq, (g + 1) * tq) if h is None else slice(g * tq + h * sub, g * tq + (h + 1) * sub)
        m_old = m_ref[sl, :]
        mx = s[:, 0:LANES]
        for j in range(1, nslab):
            mx = jnp.maximum(mx, s[:, j * LANES:(j + 1) * LANES])
        m_new = jnp.maximum(m_old, jnp.max(mx, axis=-1, keepdims=True))
        alpha = jnp.exp(m_old - m_new)
        p = jnp.concatenate([jnp.exp(s[:, j * LANES:(j + 1) * LANES] - m_new) for j in range(nslab)], axis=1)
        st_ref[sl, :] = alpha * st_ref[sl, :] + _dot(p.astype(BF16), v_aug)
        m_ref[sl, :] = m_new

    def result(state, g):
        st = state[1][g * tq:(g + 1) * tq, :]
        return st[:, 0:HEAD_DIM] * (1.0 / st[:, HEAD_DIM:HEAD_DIM + 1])

    def kpos(kj):
        return kj * tq + lax.broadcasted_iota(jnp.int32, (1, tq), 1)

    win_st = (mw_sc, stw_sc)
    reset(win_st)
    for w in range(WINDOW // tq + 1):
        kj = qi - WINDOW // tq + w
        off = pl.multiple_of(jnp.maximum(kj, 0) * tq, tq)
        kt = kwt_ref[0, :, pl.ds(off, tq)]
        v_aug = vw_ref[0, 0, pl.ds(off, tq), :]
        kp = kpos(kj)
        mask = (kp <= t_q) & (kp > t_q - WINDOW) & (kp >= 0)
        for g in range(GQA):
            online(win_st, g, jnp.where(mask, _dot(q_ref[0, g], kt), NEG_BIG), v_aug)

    n_col = lax.broadcasted_iota(jnp.int32, (1, ncp), 1)
    valid = (n_col * CMP_STRIDE + (CMP_BLOCK - 1)) <= t_q
    kct = kct_ref[0]
    vc = vc_ref[0, 0]
    o_cmp = []
    psum = None
    for g in range(GQA):
        sm = jnp.where(valid, _dot(q_ref[0, g], kct), NEG_BIG)
        e = jnp.exp(sm - jnp.max(sm, axis=-1, keepdims=True))
        p = jnp.where(valid, e * (1.0 / jnp.sum(e, axis=-1, keepdims=True)), 0.0)
        o_cmp.append(_dot(p.astype(BF16), vc))
        psum = p if g == 0 else psum + p

    amat = amat_ref[...]
    p_hi = psum.astype(BF16)
    r1 = psum - p_hi.astype(F32)
    p_mid = r1.astype(BF16)
    p_lo = (r1 - p_mid.astype(F32)).astype(BF16)
    imp = _dot(p_hi, amat) + _dot(p_mid, amat) + _dot(p_lo, amat)
    blk = lax.broadcasted_iota(jnp.int32, (nblk, 1), 0)
    t_l = t0 + lax.broadcasted_iota(jnp.int32, (1, tq), 1)
    cur = t_l >> 6
    forced = (blk == 0) | (blk == cur) | (blk == cur - 1)
    impm = jnp.where(forced, -NEG_BIG, jnp.where(blk * SEL_BLOCK <= t_l, imp.T, NEG_BIG))
    impm = impm.reshape(nblk // 8, 8, tq)
    blk_f = (lax.broadcasted_iota(jnp.int32, (nblk // 8, 8, tq), 0) * 8
             + lax.broadcasted_iota(jnp.int32, (nblk // 8, 8, tq), 1)).astype(F32)

    def over_blocks(x, op):
        r = x[0]
        for a in range(1, nblk // 8):
            r = op(r, x[a])
        for sh in (4, 2, 1):
            r = op(r, pltpu.roll(r, sh, 0))
        return r[None]

    selneg = jnp.full((nblk // 8, 8, tq), NEG_BIG, F32)
    for _ in range(min(N_SELECT, nblk)):
        best = over_blocks(impm, jnp.maximum)
        idx = over_blocks(jnp.where(impm == best, blk_f, float(nblk)), jnp.minimum)
        pick = blk_f == idx
        selneg = jnp.where(pick, 0.0, selneg)
        impm = jnp.where(pick, -jnp.inf, impm)
    selneg = selneg.reshape(nblk, tq).T.astype(BF16)

    for g in range(GQA):
        sl = slice(g * tq, (g + 1) * tq)
        lhs_sc[sl, 0:HEAD_DIM] = q_ref[0, g]
        lhs_sc[sl, HEAD_DIM:LANES] = jnp.zeros((tq, LANES - HEAD_DIM), BF16)
        lhs_sc[sl, LANES:LANES + nblk] = selneg

    sel_st = (m_sc, st_sc)
    reset(sel_st)

    def sel_scores(kj):
        rhs = rhs_sc[:, pl.ds(pl.multiple_of(kj * tq, tq), tq)]
        return tuple(_dot(lhs_sc[g * tq + h * sub:g * tq + (h + 1) * sub, :], rhs)
                     for g in range(GQA) for h in range(tq // sub))

    def sel_update(kj, scores, mask):
        v_aug = vs_ref[0, 0, pl.ds(pl.multiple_of(kj * tq, tq), tq), :]
        for g in range(GQA):
            for h in range(tq // sub):
                sc = scores[g * (tq // sub) + h]
                if mask is not None:
                    sc = jnp.where(mask[h * sub:(h + 1) * sub], sc, NEG_BIG)
                online(sel_st, g, sc, v_aug, h)

    def sel_step(kj, scores):
        nxt = sel_scores(kj + 1)
        sel_update(kj, scores, None)
        return nxt

    scores = lax.fori_loop(0, qi, sel_step, sel_scores(0))
    sel_update(qi, scores, kpos(qi) <= t_q)

    gt = gt_ref[0]
    for g in range(GQA):
        o = (gt[:, 3 * g:3 * g + 1] * o_cmp[g] + gt[:, 3 * g + 1:3 * g + 2] * result(sel_st, g)
             + gt[:, 3 * g + 2:3 * g + 3] * result(win_st, g))
        o_ref[:, g * HEAD_DIM:(g + 1) * HEAD_DIM] = o


def _importance_matrix(seq):
    ncp = seq // CMP_STRIDE
    nblk = seq // SEL_BLOCK
    ratio = SEL_BLOCK // CMP_STRIDE
    n = jnp.arange(ncp)[:, None]
    j = jnp.arange(nblk)[None, :]
    a = sum(((n == ratio * j + r).astype(F32) + (n == ratio * j + r + 1).astype(F32)) for r in range(ratio))
    return jnp.where(n < ncp - 1, a, 0.0).astype(BF16)


def _nsa_attention(q_hm, kct, vc_hm, kst, vs_hm, kwt, vw_hm, gates, bsz, seq):
    tq = min(ATT_TILE, seq)
    nq = seq // tq
    nblk = seq // SEL_BLOCK
    ncp = seq // CMP_STRIDE
    rows = GQA * tq
    amat = _importance_matrix(seq)
    head = lambda b, h, i: (b, h, 0, 0)
    headt = lambda b, h, i: (b, h, 0)
    in_specs = [pl.BlockSpec((1, GQA, tq, HEAD_DIM), lambda b, h, i: (b, h, i, 0)),
                pl.BlockSpec((1, HEAD_DIM, ncp), headt),
                pl.BlockSpec((1, 1, ncp, HEAD_DIM), head),
                pl.BlockSpec((1, HEAD_DIM, seq), headt),
                pl.BlockSpec((1, 1, seq, LANES), head),
                pl.BlockSpec((1, HEAD_DIM, seq), headt),
                pl.BlockSpec((1, 1, seq, LANES), head),
                pl.BlockSpec((1, tq, LANES), lambda b, h, i: (h, b * nq + i, 0)),
                _const_spec(amat.shape)]
    return pl.pallas_call(
        functools.partial(_nsa_kernel, tq=tq, seq=seq),
        grid=(bsz, N_KV_HEADS, nq),
        in_specs=in_specs,
        out_specs=pl.BlockSpec((tq, GQA * HEAD_DIM), lambda b, h, i: (b * nq + i, h)),
        out_shape=jax.ShapeDtypeStruct((bsz * seq, N_HEADS * HEAD_DIM), F32),
        scratch_shapes=[pltpu.VMEM((LANES + nblk, seq), BF16),
                        pltpu.VMEM((rows, LANES + nblk), BF16),
                        pltpu.VMEM((rows, LANES), F32), pltpu.VMEM((rows, LANES), F32),
                        pltpu.VMEM((rows, LANES), F32), pltpu.VMEM((rows, LANES), F32)],
        compiler_params=_params(("arbitrary", "arbitrary", "arbitrary")),
        name="nsa_attention",
    )(q_hm, kct, vc_hm, kst, vs_hm, kwt, vw_hm, gates, amat)


def _out_kernel(ys_ref, ya_ref, x_ref, wglu_ref, bglu_ref, gs_ref, ga_ref, woa_ref, wob_ref,
                ga1_ref, gf_ref, sh2_ref, sc2_ref, x1_ref, h2_ref, ysc):
    nchunk = ysc.shape[1] // S5_CHUNK
    for j in range(ysc.shape[0]):
        for k in range(S5_CHUNK):
            ysc[j, pl.ds(k, nchunk, stride=S5_CHUNK), :] = ys_ref[k, j].astype(F32)
    z = jax.nn.gelu(jnp.concatenate([ysc[j] for j in range(ysc.shape[0])], axis=1))
    glu = z * jax.nn.sigmoid(_dot(z.astype(BF16), wglu_ref[...]) + bglu_ref[...])
    n1 = _rms(glu, gs_ref[...]).astype(BF16)
    n2 = _rms(ya_ref[...], ga_ref[...]).astype(BF16)
    y = _dot(n1, woa_ref[...]) + _dot(n2, wob_ref[...])
    x1 = x_ref[...] + ga1_ref[0] * y
    x1_ref[...] = x1
    h2_ref[...] = (_rms(x1, gf_ref[...]) * (1.0 + sc2_ref[0]) + sh2_ref[0]).astype(BF16)


def _out_proj(y_ssm, y_att, x2, w_glu, b_glu, g_ssm, g_nsa, w_out, ga1, g_ffn, sh2, sc2, bsz, seq, tm):
    t, d = x2.shape
    d_att = y_att.shape[1]
    d_ssm = d - d_att
    nt = seq // tm
    row = lambda i: (i, 0)
    bat = lambda i: (i // nt, 0, 0)
    w_out = w_out.astype(BF16)
    in_specs = [pl.BlockSpec((S5_CHUNK, d_ssm // LANES, tm // S5_CHUNK, LANES), lambda i: (0, 0, i, 0)),
                pl.BlockSpec((tm, d_att), row), pl.BlockSpec((tm, d), row),
                _const_spec((d_ssm, d_ssm)), _const_spec((1, d_ssm)), _const_spec((1, d_ssm)),
                _const_spec((1, d_att)), _const_spec((d_ssm, d)), _const_spec((d_att, d)),
                pl.BlockSpec((1, 1, d), bat), _const_spec((1, d)),
                pl.BlockSpec((1, 1, d), bat), pl.BlockSpec((1, 1, d), bat)]
    return pl.pallas_call(
        _out_kernel,
        grid=(t // tm,),
        in_specs=in_specs,
        out_specs=[pl.BlockSpec((tm, d), row), pl.BlockSpec((tm, d), row)],
        out_shape=[jax.ShapeDtypeStruct((t, d), F32), jax.ShapeDtypeStruct((t, d), BF16)],
        scratch_shapes=[pltpu.VMEM((d_ssm // LANES, tm, LANES), F32)],
        compiler_params=_params(("arbitrary",)),
        name="out_proj",
    )(y_ssm, y_att, x2, w_glu.astype(BF16), b_glu.reshape(1, d_ssm), g_ssm.reshape(1, d_ssm),
      g_nsa.reshape(1, d_att), w_out[:d_ssm], w_out[d_ssm:], ga1.reshape(bsz, 1, d), g_ffn.reshape(1, d),
      sh2.reshape(bsz, 1, d), sc2.reshape(bsz, 1, d))


FFN_HALO = 16


def _ffn_kernel(h_ref, halo_ref, x1_ref, wv_ref, wg_ref, cwv_ref, cwg_ref, cbv_ref, cbg_ref, wd_ref,
                ga2_ref, gfin_ref, o_ref, act_sc, acc_sc, *, nt, nj, nsteps):
    s = pl.program_id(0)
    su = jnp.minimum(s, nsteps - 1)
    sd = jnp.maximum(s - 1, 0)
    iu = lax.div(su, nj)
    jd = lax.rem(sd, nj)
    slot = lax.rem(s, 2)

    @pl.when(s == 0)
    def _():
        act_sc[...] = jnp.zeros(act_sc.shape, BF16)
        acc_sc[...] = jnp.zeros(acc_sc.shape, F32)

    part = _dot(act_sc[1 - slot], wd_ref[...])
    acc_sc[...] = jnp.where(jd > 0, acc_sc[...], 0.0) + part

    halo = halo_ref[...]
    halo = jnp.where(lax.rem(iu, nt) == 0, jnp.zeros_like(halo), halo)
    hx = jnp.concatenate([halo, h_ref[...]], axis=0)

    def conv(w_ref, cw_ref, cb_ref):
        up = _dot(hx, w_ref[...])
        cw = cw_ref[...]
        c = (pltpu.roll(up, 2, 0) * cw[0:1] + pltpu.roll(up, 1, 0) * cw[1:2] + up * cw[2:3])
        return c[FFN_HALO:] + cb_ref[...]

    val = conv(wv_ref, cwv_ref, cbv_ref)
    gate = conv(wg_ref, cwg_ref, cbg_ref)
    act_sc[slot] = (gate * jax.nn.sigmoid(gate) * val).astype(BF16)

    @pl.when((s > 0) & (jd == nj - 1))
    def _():
        x2 = x1_ref[...] + ga2_ref[0] * acc_sc[...]
        o_ref[...] = _rms(x2, gfin_ref[...])


def _conv_ffn(h2, x1, w_up, conv_w, conv_b, w_down, ga2, g_final, bsz, seq, tm, tn):
    t, d = x1.shape
    dff = w_down.shape[0]
    nt = seq // tm
    nj = dff // tn
    nsteps = (t // tm) * nj
    w_up = w_up.astype(BF16)
    w_down = w_down.astype(BF16)
    cb = conv_b.reshape(1, 2 * dff)
    hb = tm // FFN_HALO
    iu = lambda s: lax.div(jnp.minimum(s, nsteps - 1), nj)
    ju = lambda s: lax.rem(jnp.minimum(s, nsteps - 1), nj)
    idn = lambda s: lax.div(jnp.maximum(s - 1, 0), nj)
    jdn = lambda s: lax.rem(jnp.maximum(s - 1, 0), nj)
    in_specs = [pl.BlockSpec((tm, d), lambda s: (iu(s), 0)),
                pl.BlockSpec((FFN_HALO, d), lambda s: (jnp.maximum(iu(s) * hb - 1, 0), 0)),
                pl.BlockSpec((tm, d), lambda s: (idn(s), 0)),
                pl.BlockSpec((d, tn), lambda s: (0, ju(s))),
                pl.BlockSpec((d, tn), lambda s: (0, nj + ju(s))),
                pl.BlockSpec((CONV_WIDTH, tn), lambda s: (0, ju(s))),
                pl.BlockSpec((CONV_WIDTH, tn), lambda s: (0, nj + ju(s))),
                pl.BlockSpec((1, tn), lambda s: (0, ju(s))),
                pl.BlockSpec((1, tn), lambda s: (0, nj + ju(s))),
                pl.BlockSpec((tn, d), lambda s: (jdn(s), 0)),
                pl.BlockSpec((1, 1, d), lambda s: (lax.div(idn(s), nt), 0, 0)),
                pl.BlockSpec((1, d), lambda s: (0, 0))]
    return pl.pallas_call(
        functools.partial(_ffn_kernel, nt=nt, nj=nj, nsteps=nsteps),
        grid=(nsteps + 1,),
        in_specs=in_specs,
        out_specs=pl.BlockSpec((tm, d), lambda s: (idn(s), 0)),
        out_shape=jax.ShapeDtypeStruct((t, d), F32),
        scratch_shapes=[pltpu.VMEM((2, tm, tn), BF16), pltpu.VMEM((tm, d), F32)],
        compiler_params=_params(("arbitrary",)),
        name="conv_ffn",
    )(h2, h2, x1, w_up, w_up, conv_w, conv_w, cb, cb, w_down, ga2.reshape(bsz, 1, d), g_final.reshape(1, d))


def _block(x, mod, g_mix_norm, w_in, s5_params, cmp_k, cmp_v, g_ssm_out, g_nsa_out, w_glu, b_glu, w_out,
           g_ffn_norm, w_up, conv_w, conv_b, w_down, g_final):
    bsz, seq, d = x.shape
    t = bsz * seq
    d_ssm = d // 2
    x2 = x.reshape(t, d)
    sh1, sc1, ga1, sh2, sc2, ga2 = jnp.split(mod, 6, axis=-1)
    tm = min(512, seq)

    u, q_hm, kc_tok, vc_tok, kst, vs_hm, kwt, vw_hm, gates = _input_proj(
        x2, g_mix_norm, sh1, sc1, w_in, _rope_tables(seq), bsz, seq, tm)

    y_ssm = _s5_scan(u, _s5_prep(*s5_params), bsz)

    kct = _compress(kc_tok, *cmp_k, bsz, seq, True)
    vc_hm = _compress(vc_tok, *cmp_v, bsz, seq, False)
    y_att = _nsa_attention(q_hm, kct, vc_hm, kst, vs_hm, kwt, vw_hm, gates, bsz, seq)

    x1, h2 = _out_proj(y_ssm, y_att, x2, w_glu, b_glu, g_ssm_out, g_nsa_out, w_out, ga1, g_ffn_norm,
                       sh2, sc2, bsz, seq, tm)
    out = _conv_ffn(h2, x1, w_up, conv_w, conv_b, w_down, ga2, g_final, bsz, seq, tm, 512)
    return out.reshape(bsz, seq, d)


def kernel(x, c, w_ada, b_ada, g_mix_norm, w_in, lam_re, lam_im, log_dt, b_re, b_im, c_re, c_im, d_skip,
           w_glu, b_glu, pe_k, w1_k, w2_k, pe_v, w1_v, w2_v, g_ssm_out, g_nsa_out, w_out, g_ffn_norm,
           w_up, conv_w, conv_b, w_down, g_final):
    l = 0
    mod = _adaln_mod(c, w_ada[l], b_ada[l])
    return _block(x, mod, g_mix_norm[l], w_in[l],
                  (lam_re[l], lam_im[l], log_dt[l], b_re[l], b_im[l], c_re[l], c_im[l], d_skip[l]),
                  (pe_k[l], w1_k[l], w2_k[l]), (pe_v[l], w1_v[l], w2_v[l]),
                  g_ssm_out[l], g_nsa_out[l], w_glu[l], b_glu[l], w_out[l], g_ffn_norm[l],
                  w_up[l], conv_w[l], conv_b[l], w_down[l], g_final)
```

```python
import functools
import math

import jax
import jax.numpy as jnp
from jax import lax
from jax.experimental import pallas as pl
from jax.experimental.pallas import tpu as pltpu

F32 = jnp.float32
BF16 = jnp.bfloat16
HIGHEST = lax.Precision.HIGHEST

SSM_GROUP = 16
SSM_STATE = 64
N_HEADS = 16
N_KV_HEADS = 4
HEAD_DIM = 64
GQA = N_HEADS // N_KV_HEADS
KV_DIM = N_KV_HEADS * HEAD_DIM
ROT_DIM = HEAD_DIM // 4
ROPE_THETA = 500000.0
CMP_BLOCK = 32
CMP_STRIDE = 16
CMP_HIDDEN = 2 * HEAD_DIM
SEL_BLOCK = 64
N_SELECT = 16
WINDOW = 512
CONV_WIDTH = 3
NORM_EPS = 1e-6
NEG_BIG = -1e30
LOG2_E = math.log2(math.e)

LANES = 128
S5_CHUNK = 16
ATT_TILE = 256
ATT_SUB_ROWS = 256
VMEM_LIMIT = 56 * 2**20


def _params(sem, vmem=VMEM_LIMIT):
    return pltpu.CompilerParams(dimension_semantics=sem, vmem_limit_bytes=vmem)


def _const_spec(shape):
    n = len(shape)
    return pl.BlockSpec(shape, lambda *_: (0,) * n)


def _rms(x, g):
    return x * lax.rsqrt(jnp.mean(x * x, axis=-1, keepdims=True) + NORM_EPS) * g


def _dot(a, b):
    return jnp.dot(a, b, preferred_element_type=F32)


def _dot_nt(a, b, precision=None):
    return lax.dot_general(a, b, (((1,), (1,)), ((), ())), precision=precision,
                           preferred_element_type=F32)


def _mod_kernel(c_ref, w_ref, b_ref, o_ref):
    c = c_ref[...]
    sc = c * jax.nn.sigmoid(c)
    o_ref[...] = jnp.dot(sc, w_ref[...], preferred_element_type=F32, precision=HIGHEST) + b_ref[...]


def _adaln_mod(c, w_ada, b_ada):
    bsz, d = c.shape
    n = w_ada.shape[1]
    tn = n // 8
    cp = jnp.zeros((8, d), F32).at[:bsz].set(c)
    out = pl.pallas_call(
        _mod_kernel,
        grid=(n // tn,),
        in_specs=[pl.BlockSpec((8, d), lambda j: (0, 0)),
                  pl.BlockSpec((d, tn), lambda j: (0, j)),
                  pl.BlockSpec((1, tn), lambda j: (0, j))],
        out_specs=pl.BlockSpec((8, tn), lambda j: (0, j)),
        out_shape=jax.ShapeDtypeStruct((8, n), F32),
        compiler_params=_params(("arbitrary",)),
        name="adaln_mod",
    )(cp, w_ada, b_ada.reshape(1, n))
    return out[:bsz]


def _rope_kernel(c_ref, sm_ref, sp_ref, *, tr):
    i = pl.program_id(0)
    pos = (i * tr + lax.broadcasted_iota(jnp.int32, (tr, LANES), 0)).astype(F32)
    lane = lax.broadcasted_iota(jnp.int32, (tr, LANES), 1)
    d = lane & (HEAD_DIM - 1)
    half = ROT_DIM // 2
    fi = d & (half - 1)
    inv = jnp.zeros((tr, LANES), F32)
    for k in range(half):
        inv = jnp.where(fi == k, ROPE_THETA ** (-k / half), inv)
    ang = pos * inv
    cs = jnp.cos(ang)
    sn = jnp.sin(ang)
    c_ref[...] = jnp.where(d < ROT_DIM, cs, 1.0)
    sm_ref[...] = jnp.where(d < half, -sn, 0.0)
    sp_ref[...] = jnp.where((d >= half) & (d < ROT_DIM), sn, 0.0)


def _rope_tables(seq):
    tr = min(seq, 1024)
    sds = jax.ShapeDtypeStruct((seq, LANES), F32)
    spec = pl.BlockSpec((tr, LANES), lambda i: (i, 0))
    return pl.pallas_call(
        functools.partial(_rope_kernel, tr=tr),
        grid=(seq // tr,),
        in_specs=[],
        out_specs=[spec, spec, spec],
        out_shape=[sds, sds, sds],
        compiler_params=_params(("arbitrary",)),
        name="rope_tables",
    )()


def _proj_kernel(x_ref, g_ref, sh_ref, sc_ref, wu_ref, wq_ref, wkv_ref, wg_ref, rc_ref, rm_ref, rp_ref,
                 u_ref, q_ref, kc_ref, vc_ref, kst_ref, vs_ref, kwt_ref, vw_ref, gt_ref, usc):
    x = x_ref[...]
    h = _rms(x, g_ref[...]) * (1.0 + sc_ref[0]) + sh_ref[0]
    hb = h.astype(BF16)
    rc, rm, rp = rc_ref[...], rm_ref[...], rp_ref[...]

    def rope(s):
        return s * rc + pltpu.roll(s, LANES - ROT_DIM // 2, 1) * rm + pltpu.roll(s, ROT_DIM // 2, 1) * rp

    def roped(a):
        return jnp.concatenate([rope(a[:, :LANES]), rope(a[:, LANES:])], axis=1)

    u = _dot(hb, wu_ref[...])
    nchunk = x.shape[0] // S5_CHUNK
    for j in range(u_ref.shape[1]):
        usc[j] = u[:, j * LANES:(j + 1) * LANES]
        for k in range(S5_CHUNK):
            u_ref[k, j] = usc[j, pl.ds(k, nchunk, stride=S5_CHUNK), :].astype(BF16)

    q = _dot(hb, wq_ref[...])
    scale = HEAD_DIM ** -0.5 * LOG2_E
    for j in range(N_HEADS // 2):
        s = rope(q[:, j * LANES:(j + 1) * LANES]) * scale
        q_ref[0, 2 * j] = s[:, :HEAD_DIM].astype(BF16)
        q_ref[0, 2 * j + 1] = s[:, HEAD_DIM:].astype(BF16)

    kv = _dot(hb, wkv_ref[...])

    def seg(i):
        return kv[:, i * KV_DIM:(i + 1) * KV_DIM]

    kc_ref[...] = roped(seg(0)).astype(BF16)
    vc_ref[...] = seg(1).astype(BF16)
    kst_ref[0] = roped(seg(2)).T.astype(BF16)
    vs = seg(3)
    kwt_ref[0] = roped(seg(4)).T.astype(BF16)
    vw = seg(5)
    ones_col = jnp.where(lax.broadcasted_iota(jnp.int32, (x.shape[0], LANES - HEAD_DIM), 1) == 0, 1.0, 0.0)
    for hh in range(N_KV_HEADS):
        sl = slice(hh * HEAD_DIM, (hh + 1) * HEAD_DIM)
        vs_ref[0, hh] = jnp.concatenate([vs[:, sl], ones_col], axis=1).astype(BF16)
        vw_ref[0, hh] = jnp.concatenate([vw[:, sl], ones_col], axis=1).astype(BF16)

    sg = jax.nn.sigmoid(_dot(hb, wg_ref[...]))
    gt_ref[0] = sg
    for hh in range(1, N_KV_HEADS):
        gt_ref[hh] = pltpu.roll(sg, LANES - 3 * GQA * hh, 1)


def _input_proj(x2, g, sh, sc, w_in, rope_tabs, bsz, seq, tm):
    t, d = x2.shape
    d_ssm = d // 2
    d_att = d - d_ssm
    nt = seq // tm
    w_in = w_in.astype(BF16)
    o = 0
    wu = w_in[:, o:o + d_ssm]; o += d_ssm
    wq = w_in[:, o:o + d_att]; o += d_att
    wkv = w_in[:, o:o + 6 * KV_DIM]; o += 6 * KV_DIM
    wg = jnp.zeros((d, LANES), BF16).at[:, :3 * N_HEADS].set(w_in[:, o:])
    row = lambda i: (i, 0)
    bat = lambda i: (i // nt, 0, 0)
    tab = lambda i: (i % nt, 0)
    hm = lambda i: (i // nt, 0, i % nt, 0)
    tr = lambda i: (i // nt, 0, i % nt)
    in_specs = [pl.BlockSpec((tm, d), row), _const_spec((1, d)),
                pl.BlockSpec((1, 1, d), bat), pl.BlockSpec((1, 1, d), bat),
                _const_spec(wu.shape), _const_spec(wq.shape), _const_spec(wkv.shape), _const_spec(wg.shape),
                pl.BlockSpec((tm, LANES), tab), pl.BlockSpec((tm, LANES), tab), pl.BlockSpec((tm, LANES), tab)]
    out_shape = [jax.ShapeDtypeStruct((S5_CHUNK, d_ssm // LANES, t // S5_CHUNK, LANES), BF16),
                 jax.ShapeDtypeStruct((bsz, N_HEADS, seq, HEAD_DIM), BF16),
                 jax.ShapeDtypeStruct((t, KV_DIM), BF16),
                 jax.ShapeDtypeStruct((t, KV_DIM), BF16),
                 jax.ShapeDtypeStruct((bsz, KV_DIM, seq), BF16),
                 jax.ShapeDtypeStruct((bsz, N_KV_HEADS, seq, LANES), BF16),
                 jax.ShapeDtypeStruct((bsz, KV_DIM, seq), BF16),
                 jax.ShapeDtypeStruct((bsz, N_KV_HEADS, seq, LANES), BF16),
                 jax.ShapeDtypeStruct((N_KV_HEADS, t, LANES), F32)]
    out_specs = [pl.BlockSpec((S5_CHUNK, d_ssm // LANES, tm // S5_CHUNK, LANES), lambda i: (0, 0, i, 0)),
                 pl.BlockSpec((1, N_HEADS, tm, HEAD_DIM), hm),
                 pl.BlockSpec((tm, KV_DIM), row),
                 pl.BlockSpec((tm, KV_DIM), row),
                 pl.BlockSpec((1, KV_DIM, tm), tr),
                 pl.BlockSpec((1, N_KV_HEADS, tm, LANES), hm),
                 pl.BlockSpec((1, KV_DIM, tm), tr),
                 pl.BlockSpec((1, N_KV_HEADS, tm, LANES), hm),
                 pl.BlockSpec((N_KV_HEADS, tm, LANES), lambda i: (0, i, 0))]
    return pl.pallas_call(
        _proj_kernel,
        grid=(t // tm,),
        in_specs=in_specs, out_specs=out_specs, out_shape=out_shape,
        scratch_shapes=[pltpu.VMEM((d_ssm // LANES, tm, LANES), F32)],
        compiler_params=_params(("arbitrary",)),
        name="input_proj",
    )(x2, g.reshape(1, d), sh.reshape(bsz, 1, d), sc.reshape(bsz, 1, d), wu, wq, wkv, wg, *rope_tabs)


def _s5prep_kernel(lr_ref, li_ref, ldt_ref, btr_ref, bti_ref, cr_ref, ci_ref, dsk_ref,
                   m_ref, bzr_ref, bzi_ref, cyr_ref, cyi_ref, ar_ref, ai_ref):
    L, H, P = S5_CHUNK, SSM_GROUP, SSM_STATE
    lr = lr_ref[0]
    li = li_ref[0]
    dt = jnp.exp(ldt_ref[0])
    mag = jnp.exp(lr * dt)
    ab_re = mag * jnp.cos(li * dt)
    ab_im = mag * jnp.sin(li * dt)
    den = lr * lr + li * li
    num_re = ab_re - 1.0
    f_re = (num_re * lr + ab_im * li) / den
    f_im = (ab_im * lr - num_re * li) / den
    btr = btr_ref[0]
    bti = bti_ref[0]
    bb_re = f_re * btr - f_im * bti
    bb_im = f_re * bti + f_im * btr
    c_re = cr_ref[0]
    c_im = ci_ref[0]

    r = lax.broadcasted_iota(jnp.int32, (4 * L, 1), 0)
    k = r & (L - 1)
    blk = r >> 4
    e = jnp.where(blk == 0, -k, jnp.where(blk == 1, k, jnp.where(blk == 2, L - 1 - k, k + 1))).astype(F32)
    pm = jnp.exp(lr * dt * e)
    ph = li * dt * e
    pr = pm * jnp.cos(ph)
    pi = pm * jnp.sin(ph)

    def rows(i, sm_r, sm_i):
        a_r = jnp.broadcast_to(pr[i * L:(i + 1) * L][:, None, :], (L, H, P)).reshape(L * H, P)
        a_i = jnp.broadcast_to(pi[i * L:(i + 1) * L][:, None, :], (L, H, P)).reshape(L * H, P)
        s_r = jnp.broadcast_to(sm_r[None], (L, H, P)).reshape(L * H, P)
        s_i = jnp.broadcast_to(sm_i[None], (L, H, P)).reshape(L * H, P)
        return a_r * s_r - a_i * s_i, a_r * s_i + a_i * s_r

    x_re, x_im = rows(0, bb_re, bb_im)
    y_re, y_im = rows(1, c_re, c_im)
    z_re, z_im = rows(2, bb_re, bb_im)
    w_re, w_im = rows(3, c_re, c_im)
    mm = _dot_nt(x_re, y_re, HIGHEST) - _dot_nt(x_im, y_im, HIGHEST)
    ri = lax.broadcasted_iota(jnp.int32, (L * H, L * H), 0)
    ci = lax.broadcasted_iota(jnp.int32, (L * H, L * H), 1)
    mm = jnp.where((ci >> 4) >= (ri >> 4), mm, 0.0) + jnp.where(ri == ci, dsk_ref[0], 0.0)
    m_ref[0] = mm.astype(BF16)
    bzr_ref[0] = z_re.astype(BF16)
    bzi_ref[0] = z_im.astype(BF16)
    cyr_ref[0] = w_re.astype(BF16)
    cyi_ref[0] = (-w_im).astype(BF16)
    m16 = jnp.exp(lr * dt * L)
    ar_ref[0] = m16 * jnp.cos(li * dt * L)
    ai_ref[0] = m16 * jnp.sin(li * dt * L)


def _s5_prep(lam_re, lam_im, log_dt, b_re, b_im, c_re, c_im, d_skip):
    g, p, h = b_re.shape
    lh = S5_CHUNK * h
    g3 = lambda i: (i, 0, 0)
    ins = [lam_re.reshape(g, 1, p), lam_im.reshape(g, 1, p),
           jnp.broadcast_to(log_dt.reshape(g, 1, 1), (g, 1, p)),
           b_re.transpose(0, 2, 1), b_im.transpose(0, 2, 1), c_re, c_im,
           jnp.tile(d_skip, (1, S5_CHUNK)).reshape(g, 1, lh)]
    in_specs = [pl.BlockSpec((1,) + a.shape[1:], g3) for a in ins]
    out_shape = [jax.ShapeDtypeStruct((g, lh, lh), BF16)] + \
                [jax.ShapeDtypeStruct((g, lh, p), BF16)] * 4 + \
                [jax.ShapeDtypeStruct((g, 1, p), F32)] * 2
    out_specs = [pl.BlockSpec((1,) + s.shape[1:], g3) for s in out_shape]
    return pl.pallas_call(
        _s5prep_kernel, grid=(g,), in_specs=in_specs, out_specs=out_specs, out_shape=out_shape,
        compiler_params=_params(("arbitrary",)), name="s5_prep",
    )(*ins)


def _s5_kernel(u_ref, m_ref, bzr_ref, bzi_ref, cyr_ref, cyi_ref, ar_ref, ai_ref, y_ref,
               zr_sc, zi_sc, xr_sc, xi_sc, *, nch):
    L = S5_CHUNK
    xcat = jnp.concatenate([u_ref[k, 0] for k in range(L)], axis=1)
    zr_sc[...] = _dot(xcat, bzr_ref[0])
    zi_sc[...] = _dot(xcat, bzi_ref[0])
    a_r = ar_ref[0]
    a_i = ai_ref[0]

    def body(c, carry):
        x_r, x_i = carry
        xr_sc[pl.ds(c, 1), :] = x_r
        xi_sc[pl.ds(c, 1), :] = x_i
        z_r = zr_sc[pl.ds(c, 1), :]
        z_i = zi_sc[pl.ds(c, 1), :]
        return a_r * x_r - a_i * x_i + z_r, a_r * x_i + a_i * x_r + z_i

    zero = jnp.zeros((1, zr_sc.shape[1]), F32)
    lax.fori_loop(0, nch, body, (zero, zero))
    x_r = xr_sc[...].astype(BF16)
    x_i = xi_sc[...].astype(BF16)
    tw = 2 * LANES
    for n in range(L // 2):
        kk = tw * (n + 1)
        cols = slice(n * tw, (n + 1) * tw)
        y = (_dot(xcat[:, :kk], m_ref[0, :kk, cols]) + _dot(x_r, cyr_ref[0, :, cols])
             + _dot(x_i, cyi_ref[0, :, cols]))
        y_ref[2 * n, 0] = y[:, :LANES].astype(BF16)
        y_ref[2 * n + 1, 0] = y[:, LANES:].astype(BF16)


def _s5_block_kernel(m_ref, bzr_ref, bzi_ref, cyr_ref, cyi_ref, m2_ref, bzr2_ref, bzi2_ref, cyr2_ref, cyi2_ref):
    L, H, P = S5_CHUNK, SSM_GROUP, SSM_STATE
    gb = LANES // H
    w = L * LANES
    r = lax.broadcasted_iota(jnp.int32, (L * H, w), 0)
    q = lax.broadcasted_iota(jnp.int32, (L * H, w), 1)
    same_tc = ((r >> 4) == (q >> 7)) & ((r & (H - 1)) == (q & (H - 1)))
    q_grp = (q >> 4) & (gb - 1)
    sr = lax.broadcasted_iota(jnp.int32, (P, gb * P), 0)
    sq = lax.broadcasted_iota(jnp.int32, (P, gb * P), 1)
    eye_p = jnp.where(lax.broadcasted_iota(jnp.int32, (P, P), 0) == lax.broadcasted_iota(jnp.int32, (P, P), 1),
                      1.0, 0.0).astype(BF16)
    for g in range(gb):
        spread_c = jnp.where(same_tc & (q_grp == g), 1.0, 0.0).astype(BF16)
        spread_s = jnp.where(sq - g * P == sr, 1.0, 0.0).astype(BF16)
        xm = _dot(m_ref[g], spread_c).astype(BF16)
        xr = _dot(bzr_ref[g], spread_s).astype(BF16)
        xi = _dot(bzi_ref[g], spread_s).astype(BF16)
        for k in range(L):
            rows = slice(k * LANES + g * H, k * LANES + (g + 1) * H)
            m2_ref[0, rows, :] = xm[k * H:(k + 1) * H, :]
            bzr2_ref[0, rows, :] = xr[k * H:(k + 1) * H, :]
            bzi2_ref[0, rows, :] = xi[k * H:(k + 1) * H, :]
        for src, dst in ((cyr_ref, cyr2_ref), (cyi_ref, cyi2_ref)):
            cy_t = _dot_nt(eye_p, src[g]).astype(BF16)
            dst[0, g * P:(g + 1) * P, :] = _dot(cy_t, spread_c).astype(BF16)


def _s5_block_operators(ops, gb):
    m, bzr, bzi, cyr, cyi, a_r, a_i = ops
    g, lh, p = bzr.shape
    nb = g // gb
    w = S5_CHUNK * LANES
    sw = gb * p
    g3 = lambda j: (j, 0, 0)
    out_shape = [jax.ShapeDtypeStruct((nb, w, w), BF16), jax.ShapeDtypeStruct((nb, w, sw), BF16),
                 jax.ShapeDtypeStruct((nb, w, sw), BF16), jax.ShapeDtypeStruct((nb, sw, w), BF16),
                 jax.ShapeDtypeStruct((nb, sw, w), BF16)]
    m2, bzr2, bzi2, cyr2, cyi2 = pl.pallas_call(
        _s5_block_kernel,
        grid=(nb,),
        in_specs=[pl.BlockSpec((gb, lh, lh), g3)] + [pl.BlockSpec((gb, lh, p), g3)] * 4,
        out_specs=[pl.BlockSpec((1,) + s.shape[1:], g3) for s in out_shape],
        out_shape=out_shape,
        compiler_params=_params(("arbitrary",)),
        name="s5_block_ops",
    )(m, bzr, bzi, cyr, cyi)
    return m2, bzr2, bzi2, cyr2, cyi2, a_r.reshape(nb, 1, sw), a_i.reshape(nb, 1, sw)


def _s5_scan(u_k, ops, bsz):
    L, nb, rows, _ = u_k.shape
    gb = LANES // SSM_GROUP
    m2, bzr, bzi, cyr, cyi, a_r, a_i = _s5_block_operators(ops, gb)
    nch = rows // bsz
    w = L * LANES
    sw = gb * SSM_STATE
    op3 = lambda j, b: (j, 0, 0)
    xspec = pl.BlockSpec((L, 1, nch, LANES), lambda j, b: (0, j, b, 0))
    in_specs = [xspec, pl.BlockSpec((1, w, w), op3),
                pl.BlockSpec((1, w, sw), op3), pl.BlockSpec((1, w, sw), op3),
                pl.BlockSpec((1, sw, w), op3), pl.BlockSpec((1, sw, w), op3),
                pl.BlockSpec((1, 1, sw), op3), pl.BlockSpec((1, 1, sw), op3)]
    return pl.pallas_call(
        functools.partial(_s5_kernel, nch=nch),
        grid=(nb, bsz),
        in_specs=in_specs,
        out_specs=xspec,
        out_shape=jax.ShapeDtypeStruct(u_k.shape, BF16),
        scratch_shapes=[pltpu.VMEM((nch, sw), F32)] * 4,
        compiler_params=_params(("arbitrary", "arbitrary")),
        name="s5_scan",
    )(u_k, m2, bzr, bzi, cyr, cyi, a_r, a_i)


def _cmp_kernel(tok_ref, w1x_ref, pe_ref, w1_ref, w2x_ref, o_ref, *, transpose_out):
    nh = N_KV_HEADS * CMP_HIDDEN
    g = _dot(tok_ref[...], w1x_ref[...])
    rows = g.shape[0]
    bias = jnp.dot(jnp.broadcast_to(pe_ref[...], (8, pe_ref.shape[1])), w1_ref[...],
                   preferred_element_type=F32, precision=HIGHEST)[0:1]
    bias = jnp.concatenate([bias] * N_KV_HEADS, axis=1)
    hid = g[:, :nh] + pltpu.roll(g[:, nh:], rows - 1, 0) + bias
    out = _dot(jax.nn.gelu(hid).astype(BF16), w2x_ref[...])
    if transpose_out:
        o_ref[0] = out.T.astype(BF16)
    else:
        for hh in range(N_KV_HEADS):
            o_ref[0, hh] = out[:, hh * HEAD_DIM:(hh + 1) * HEAD_DIM].astype(BF16)


def _compress(tok, pe, w1, w2, bsz, seq, transpose_out):
    ncp = seq // CMP_STRIDE
    half = CMP_STRIDE * HEAD_DIM
    eye = jnp.eye(N_KV_HEADS, dtype=F32)
    w1r = w1.reshape(2, CMP_STRIDE, HEAD_DIM, CMP_HIDDEN)
    w1x = jnp.einsum('zkdj,hg->khdzgj', w1r, eye).reshape(CMP_STRIDE * KV_DIM, 2 * N_KV_HEADS * CMP_HIDDEN)
    w2x = jnp.einsum('jd,hg->hjgd', w2, eye).reshape(N_KV_HEADS * CMP_HIDDEN, KV_DIM)
    tok2 = tok.reshape(bsz * ncp, CMP_STRIDE * KV_DIM)
    if transpose_out:
        out_shape = jax.ShapeDtypeStruct((bsz, KV_DIM, ncp), BF16)
        out_spec = pl.BlockSpec((1, KV_DIM, ncp), lambda b: (b, 0, 0))
    else:
        out_shape = jax.ShapeDtypeStruct((bsz, N_KV_HEADS, ncp, HEAD_DIM), BF16)
        out_spec = pl.BlockSpec((1, N_KV_HEADS, ncp, HEAD_DIM), lambda b: (b, 0, 0, 0))
    return pl.pallas_call(
        functools.partial(_cmp_kernel, transpose_out=transpose_out),
        grid=(bsz,),
        in_specs=[pl.BlockSpec((ncp, CMP_STRIDE * KV_DIM), lambda b: (b, 0)),
                  _const_spec(w1x.shape), _const_spec((1, 2 * half)), _const_spec(w1.shape),
                  _const_spec(w2x.shape)],
        out_specs=out_spec, out_shape=out_shape,
        compiler_params=_params(("arbitrary",)),
        name="compress_k" if transpose_out else "compress_v",
    )(tok2, w1x.astype(BF16), pe.reshape(1, 2 * half), w1, w2x.astype(BF16))


def _nsa_kernel(q_ref, kct_ref, vc_ref, kst_ref, vs_ref, kwt_ref, vw_ref, gt_ref, amat_ref, o_ref,
                rhs_sc, lhs_sc, m_sc, st_sc, mw_sc, stw_sc, *, tq, seq):
    nblk = seq // SEL_BLOCK
    ncp = seq // CMP_STRIDE
    rows = GQA * tq
    nslab = tq // LANES
    sub = min(ATT_SUB_ROWS, tq)
    qi = pl.program_id(2)
    t0 = qi * tq

    @pl.when(qi == 0)
    def _():
        rhs_sc[0:HEAD_DIM, :] = kst_ref[0]
        rhs_sc[HEAD_DIM:LANES, :] = jnp.zeros((LANES - HEAD_DIM, seq), BF16)
        cw = min(seq, 1024)

        def fill(c, _):
            col = c * cw + lax.broadcasted_iota(jnp.int32, (nblk, cw), 1)
            blk = lax.broadcasted_iota(jnp.int32, (nblk, cw), 0)
            onehot = jnp.where((col >> 6) == blk, 1.0, 0.0).astype(BF16)
            rhs_sc[LANES:LANES + nblk, pl.ds(pl.multiple_of(c * cw, cw), cw)] = onehot
            return 0

        lax.fori_loop(0, seq // cw, fill, 0)

    t_q = t0 + lax.broadcasted_iota(jnp.int32, (tq, 1), 0)

    def reset(state):
        state[0][...] = jnp.full((rows, LANES), NEG_BIG, F32)
        state[1][...] = jnp.zeros((rows, LANES), F32)

    def online(state, g, s, v_aug, h=None):
        m_ref, st_ref = state
        sl = slice(g * tq, (g + 1) * tq) if h is None else slice(g * tq + h * sub, g * tq + (h + 1) * sub)
        m_old = m_ref[sl, :]
        mx = s[:, 0:LANES]
        for j in range(1, nslab):
            mx = jnp.maximum(mx, s[:, j * LANES:(j + 1) * LANES])
        m_new = jnp.maximum(m_old, jnp.max(mx, axis=-1, keepdims=True))
        alpha = jnp.exp2(m_old - m_new)
        p = jnp.concatenate([jnp.exp2(s[:, j * LANES:(j + 1) * LANES] - m_new) for j in range(nslab)], axis=1)
        st_ref[sl, :] = alpha * st_ref[sl, :] + _dot(p.astype(BF16), v_aug)
        m_ref[sl, :] = m_new

    def result(state, g):
        st = state[1][g * tq:(g + 1) * tq, :]
        return st[:, 0:HEAD_DIM] * (1.0 / st[:, HEAD_DIM:HEAD_DIM + 1])

    def kpos(kj):
        return kj * tq + lax.broadcasted_iota(jnp.int32, (1, tq), 1)

    win_st = (mw_sc, stw_sc)
    reset(win_st)
    for w in range(WINDOW // tq + 1):
        kj = qi - WINDOW // tq + w
        off = pl.multiple_of(jnp.maximum(kj, 0) * tq, tq)
        kt = kwt_ref[0, :, pl.ds(off, tq)]
        v_aug = vw_ref[0, 0, pl.ds(off, tq), :]
        kp = kpos(kj)
        if w == WINDOW // tq:
            mask = kp <= t_q
        elif w == 0:
            mask = (kp > t_q - WINDOW) & (kp >= 0)
        else:
            mask = jnp.broadcast_to(kp >= 0, (tq, tq))
        for g in range(GQA):
            online(win_st, g, jnp.where(mask, _dot(q_ref[0, g], kt), NEG_BIG), v_aug)

    n_col = lax.broadcasted_iota(jnp.int32, (1, ncp), 1)
    valid = (n_col * CMP_STRIDE + (CMP_BLOCK - 1)) <= t_q
    row_live = t_q >= CMP_BLOCK - 1
    kct = kct_ref[0]
    vc = vc_ref[0, 0]
    o_cmp = []
    psum = None
    for g in range(GQA):
        sm = jnp.where(valid, _dot(q_ref[0, g], kct), NEG_BIG)
        e = jnp.exp2(sm - jnp.max(sm, axis=-1, keepdims=True))
        p = e * jnp.where(row_live, 1.0 / jnp.sum(e, axis=-1, keepdims=True), 0.0)
        o_cmp.append(_dot(p.astype(BF16), vc))
        psum = p if g == 0 else psum + p

    amat = amat_ref[...]
    p_hi = psum.astype(BF16)
    r1 = psum - p_hi.astype(F32)
    p_mid = r1.astype(BF16)
    p_lo = (r1 - p_mid.astype(F32)).astype(BF16)
    imp = _dot(p_hi, amat) + _dot(p_mid, amat) + _dot(p_lo, amat)
    blk = lax.broadcasted_iota(jnp.int32, (nblk, 1), 0)
    t_l = t0 + lax.broadcasted_iota(jnp.int32, (1, tq), 1)
    cur = t_l >> 6
    forced = (blk == 0) | (blk == cur) | (blk == cur - 1)
    impm = jnp.where(forced, -NEG_BIG, jnp.where(blk * SEL_BLOCK <= t_l, imp.T, NEG_BIG))
    impm = impm.reshape(nblk // 8, 8, tq)
    blk_f = (lax.broadcasted_iota(jnp.int32, (nblk // 8, 8, tq), 0) * 8
             + lax.broadcasted_iota(jnp.int32, (nblk // 8, 8, tq), 1)).astype(F32)

    def over_blocks(x, op):
        r = x[0]
        for a in range(1, nblk // 8):
            r = op(r, x[a])
        for sh in (4, 2, 1):
            r = op(r, pltpu.roll(r, sh, 0))
        return r[None]

    selneg = jnp.full((nblk // 8, 8, tq), NEG_BIG, F32)
    for _ in range(min(N_SELECT, nblk)):
        best = over_blocks(impm, jnp.maximum)
        idx = over_blocks(jnp.where(impm == best, blk_f, float(nblk)), jnp.minimum)
        pick = blk_f == idx
        selneg = jnp.where(pick, 0.0, selneg)
        impm = jnp.where(pick, -jnp.inf, impm)
    selneg = selneg.reshape(nblk, tq).T.astype(BF16)

    for g in range(GQA):
        sl = slice(g * tq, (g + 1) * tq)
        lhs_sc[sl, 0:HEAD_DIM] = q_ref[0, g]
        lhs_sc[sl, HEAD_DIM:LANES] = jnp.zeros((tq, LANES - HEAD_DIM), BF16)
        lhs_sc[sl, LANES:LANES + nblk] = selneg

    sel_st = (m_sc, st_sc)
    reset(sel_st)

    def sel_scores(kj):
        rhs = rhs_sc[:, pl.ds(pl.multiple_of(kj * tq, tq), tq)]
        return tuple(_dot(lhs_sc[g * tq + h * sub:g * tq + (h + 1) * sub, :], rhs)
                     for g in range(GQA) for h in range(tq // sub))

    def sel_update(kj, scores, mask):
        v_aug = vs_ref[0, 0, pl.ds(pl.multiple_of(kj * tq, tq), tq), :]
        for g in range(GQA):
            for h in range(tq // sub):
                sc = scores[g * (tq // sub) + h]
                if mask is not None:
                    sc = jnp.where(mask[h * sub:(h + 1) * sub], sc, NEG_BIG)
                online(sel_st, g, sc, v_aug, h)

    def sel_step(kj, scores):
        nxt = sel_scores(kj + 1)
        sel_update(kj, scores, None)
        return nxt

    scores = lax.fori_loop(0, qi, sel_step, sel_scores(0))
    sel_update(qi, scores, kpos(qi) <= t_q)

    gt = gt_ref[0]
    for g in range(GQA):
        o = (gt[:, 3 * g:3 * g + 1] * o_cmp[g] + gt[:, 3 * g + 1:3 * g + 2] * result(sel_st, g)
             + gt[:, 3 * g + 2:3 * g + 3] * result(win_st, g))
        o_ref[:, g * HEAD_DIM:(g + 1) * HEAD_DIM] = o


def _importance_matrix(seq):
    ncp = seq // CMP_STRIDE
    nblk = seq // SEL_BLOCK
    ratio = SEL_BLOCK // CMP_STRIDE
    n = jnp.arange(ncp)[:, None]
    j = jnp.arange(nblk)[None, :]
    a = sum(((n == ratio * j + r).astype(F32) + (n == ratio * j + r + 1).astype(F32)) for r in range(ratio))
    return jnp.where(n < ncp - 1, a, 0.0).astype(BF16)


def _nsa_attention(q_hm, kct, vc_hm, kst, vs_hm, kwt, vw_hm, gates, bsz, seq):
    tq = min(ATT_TILE, seq)
    nq = seq // tq
    nblk = seq // SEL_BLOCK
    ncp = seq // CMP_STRIDE
    rows = GQA * tq
    amat = _importance_matrix(seq)
    head = lambda b, h, i: (b, h, 0, 0)
    headt = lambda b, h, i: (b, h, 0)
    in_specs = [pl.BlockSpec((1, GQA, tq, HEAD_DIM), lambda b, h, i: (b, h, i, 0)),
                pl.BlockSpec((1, HEAD_DIM, ncp), headt),
                pl.BlockSpec((1, 1, ncp, HEAD_DIM), head),
                pl.BlockSpec((1, HEAD_DIM, seq), headt),
                pl.BlockSpec((1, 1, seq, LANES), head),
                pl.BlockSpec((1, HEAD_DIM, seq), headt),
                pl.BlockSpec((1, 1, seq, LANES), head),
                pl.BlockSpec((1, tq, LANES), lambda b, h, i: (h, b * nq + i, 0)),
                _const_spec(amat.shape)]
    return pl.pallas_call(
        functools.partial(_nsa_kernel, tq=tq, seq=seq),
        grid=(bsz, N_KV_HEADS, nq),
        in_specs=in_specs,
        out_specs=pl.BlockSpec((tq, GQA * HEAD_DIM), lambda b, h, i: (b * nq + i, h)),
        out_shape=jax.ShapeDtypeStruct((bsz * seq, N_HEADS * HEAD_DIM), F32),
        scratch_shapes=[pltpu.VMEM((LANES + nblk, seq), BF16),
                        pltpu.VMEM((rows, LANES + nblk), BF16),
                        pltpu.VMEM((rows, LANES), F32), pltpu.VMEM((rows, LANES), F32),
                        pltpu.VMEM((rows, LANES), F32), pltpu.VMEM((rows, LANES), F32)],
        compiler_params=_params(("arbitrary", "arbitrary", "arbitrary")),
        name="nsa_attention",
    )(q_hm, kct, vc_hm, kst, vs_hm, kwt, vw_hm, gates, amat)


def _out_kernel(ys_ref, ya_ref, x_ref, wglu_ref, bglu_ref, gs_ref, ga_ref, woa_ref, wob_ref,
                ga1_ref, gf_ref, sh2_ref, sc2_ref, x1_ref, h2_ref, ysc):
    nchunk = ysc.shape[1] // S5_CHUNK
    for j in range(ysc.shape[0]):
        for k in range(S5_CHUNK):
            ysc[j, pl.ds(k, nchunk, stride=S5_CHUNK), :] = ys_ref[k, j].astype(F32)
    z = jax.nn.gelu(jnp.concatenate([ysc[j] for j in range(ysc.shape[0])], axis=1))
    glu = z * jax.nn.sigmoid(_dot(z.astype(BF16), wglu_ref[...]) + bglu_ref[...])
    n1 = _rms(glu, gs_ref[...]).astype(BF16)
    n2 = _rms(ya_ref[...], ga_ref[...]).astype(BF16)
    y = _dot(n1, woa_ref[...]) + _dot(n2, wob_ref[...])
    x1 = x_ref[...] + ga1_ref[0] * y
    x1_ref[...] = x1
    h2_ref[...] = (_rms(x1, gf_ref[...]) * (1.0 + sc2_ref[0]) + sh2_ref[0]).astype(BF16)


def _out_proj(y_ssm, y_att, x2, w_glu, b_glu, g_ssm, g_nsa, w_out, ga1, g_ffn, sh2, sc2, bsz, seq, tm):
    t, d = x2.shape
    d_att = y_att.shape[1]
    d_ssm = d - d_att
    nt = seq // tm
    row = lambda i: (i, 0)
    bat = lambda i: (i // nt, 0, 0)
    w_out = w_out.astype(BF16)
    in_specs = [pl.BlockSpec((S5_CHUNK, d_ssm // LANES, tm // S5_CHUNK, LANES), lambda i: (0, 0, i, 0)),
                pl.BlockSpec((tm, d_att), row), pl.BlockSpec((tm, d), row),
                _const_spec((d_ssm, d_ssm)), _const_spec((1, d_ssm)), _const_spec((1, d_ssm)),
                _const_spec((1, d_att)), _const_spec((d_ssm, d)), _const_spec((d_att, d)),
                pl.BlockSpec((1, 1, d), bat), _const_spec((1, d)),
                pl.BlockSpec((1, 1, d), bat), pl.BlockSpec((1, 1, d), bat)]
    return pl.pallas_call(
        _out_kernel,
        grid=(t // tm,),
        in_specs=in_specs,
        out_specs=[pl.BlockSpec((tm, d), row), pl.BlockSpec((tm, d), row)],
        out_shape=[jax.ShapeDtypeStruct((t, d), F32), jax.ShapeDtypeStruct((t, d), BF16)],
        scratch_shapes=[pltpu.VMEM((d_ssm // LANES, tm, LANES), F32)],
        compiler_params=_params(("arbitrary",)),
        name="out_proj",
    )(y_ssm, y_att, x2, w_glu.astype(BF16), b_glu.reshape(1, d_ssm), g_ssm.reshape(1, d_ssm),
      g_nsa.reshape(1, d_att), w_out[:d_ssm], w_out[d_ssm:], ga1.reshape(bsz, 1, d), g_ffn.reshape(1, d),
      sh2.reshape(bsz, 1, d), sc2.reshape(bsz, 1, d))


FFN_HALO = 16
FFN_COL = 256


def _ffn_kernel(h_ref, halo_ref, x1_ref, wv_ref, wg_ref, cwv_ref, cwg_ref, cbv_ref, cbg_ref, wd_ref,
                ga2_ref, gfin_ref, o_ref, up_a, up_b, acc_sc, *, nt, nj, nsteps):
    s = pl.program_id(0)
    su = jnp.minimum(s, nsteps - 1)
    sd = jnp.maximum(s - 1, 0)
    iu = lax.div(su, nj)
    jd = lax.rem(sd, nj)

    @pl.when(s == 0)
    def _():
        up_a[...] = jnp.zeros(up_a.shape, F32)
        up_b[...] = jnp.zeros(up_b.shape, F32)
        acc_sc[...] = jnp.zeros(acc_sc.shape, F32)

    def step(up_prev, up_next):
        def conv(i, cw_ref, cb_ref, cols):
            up = up_prev[i, :, cols]
            cw = cw_ref[:, cols]
            c = (pltpu.roll(up, 2, 0) * cw[0:1] + pltpu.roll(up, 1, 0) * cw[1:2] + up * cw[2:3])
            return c[FFN_HALO:] + cb_ref[:, cols]

        halo = halo_ref[...]
        halo = jnp.where(lax.rem(iu, nt) == 0, jnp.zeros_like(halo), halo)
        hx = jnp.concatenate([halo, h_ref[...]], axis=0)

        part = None
        for c0 in range(0, wv_ref.shape[1], FFN_COL):
            cols = slice(c0, c0 + FFN_COL)
            up_next[0, :, cols] = _dot(hx, wv_ref[:, cols])
            val = conv(0, cwv_ref, cbv_ref, cols)
            gate = conv(1, cwg_ref, cbg_ref, cols)
            act = (gate * jax.nn.sigmoid(gate) * val).astype(BF16)
            up_next[1, :, cols] = _dot(hx, wg_ref[:, cols])
            d = _dot(act, wd_ref[cols, :])
            part = d if part is None else part + d
        acc_sc[...] = jnp.where(jd > 0, acc_sc[...], 0.0) + part

    odd = lax.rem(s, 2)

    @pl.when(odd == 0)
    def _():
        step(up_b, up_a)

    @pl.when(odd == 1)
    def _():
        step(up_a, up_b)

    @pl.when((s > 0) & (jd == nj - 1))
    def _():
        x2 = x1_ref[...] + ga2_ref[0] * acc_sc[...]
        o_ref[...] = _rms(x2, gfin_ref[...])


def _conv_ffn(h2, x1, w_up, conv_w, conv_b, w_down, ga2, g_final, bsz, seq, tm, tn):
    t, d = x1.shape
    dff = w_down.shape[0]
    nt = seq // tm
    nj = dff // tn
    nsteps = (t // tm) * nj
    w_up = w_up.astype(BF16)
    w_down = w_down.astype(BF16)
    cb = conv_b.reshape(1, 2 * dff)
    hb = tm // FFN_HALO
    iu = lambda s: lax.div(jnp.minimum(s, nsteps - 1), nj)
    ju = lambda s: lax.rem(jnp.minimum(s, nsteps - 1), nj)
    idn = lambda s: lax.div(jnp.maximum(s - 1, 0), nj)
    jdn = lambda s: lax.rem(jnp.maximum(s - 1, 0), nj)
    in_specs = [pl.BlockSpec((tm, d), lambda s: (iu(s), 0)),
                pl.BlockSpec((FFN_HALO, d), lambda s: (jnp.maximum(iu(s) * hb - 1, 0), 0)),
                pl.BlockSpec((tm, d), lambda s: (idn(s), 0)),
                pl.BlockSpec((d, tn), lambda s: (0, ju(s))),
                pl.BlockSpec((d, tn), lambda s: (0, nj + ju(s))),
                pl.BlockSpec((CONV_WIDTH, tn), lambda s: (0, jdn(s))),
                pl.BlockSpec((CONV_WIDTH, tn), lambda s: (0, nj + jdn(s))),
                pl.BlockSpec((1, tn), lambda s: (0, jdn(s))),
                pl.BlockSpec((1, tn), lambda s: (0, nj + jdn(s))),
                pl.BlockSpec((tn, d), lambda s: (jdn(s), 0)),
                pl.BlockSpec((1, 1, d), lambda s: (lax.div(idn(s), nt), 0, 0)),
                pl.BlockSpec((1, d), lambda s: (0, 0))]
    return pl.pallas_call(
        functools.partial(_ffn_kernel, nt=nt, nj=nj, nsteps=nsteps),
        grid=(nsteps + 1,),
        in_specs=in_specs,
        out_specs=pl.BlockSpec((tm, d), lambda s: (idn(s), 0)),
        out_shape=jax.ShapeDtypeStruct((t, d), F32),
        scratch_shapes=[pltpu.VMEM((2, FFN_HALO + tm, tn), F32), pltpu.VMEM((2, FFN_HALO + tm, tn), F32),
                        pltpu.VMEM((tm, d), F32)],
        compiler_params=_params(("arbitrary",)),
        name="conv_ffn",
    )(h2, h2, x1, w_up, w_up, conv_w, conv_w, cb, cb, w_down, ga2.reshape(bsz, 1, d), g_final.reshape(1, d))


def _block(x, mod, g_mix_norm, w_in, s5_params, cmp_k, cmp_v, g_ssm_out, g_nsa_out, w_glu, b_glu, w_out,
           g_ffn_norm, w_up, conv_w, conv_b, w_down, g_final):
    bsz, seq, d = x.shape
    t = bsz * seq
    d_ssm = d // 2
    x2 = x.reshape(t, d)
    sh1, sc1, ga1, sh2, sc2, ga2 = jnp.split(mod, 6, axis=-1)
    tm = min(512, seq)

    u, q_hm, kc_tok, vc_tok, kst, vs_hm, kwt, vw_hm, gates = _input_proj(
        x2, g_mix_norm, sh1, sc1, w_in, _rope_tables(seq), bsz, seq, tm)

    y_ssm = _s5_scan(u, _s5_prep(*s5_params), bsz)

    kct = _compress(kc_tok, *cmp_k, bsz, seq, True)
    vc_hm = _compress(vc_tok, *cmp_v, bsz, seq, False)
    y_att = _nsa_attention(q_hm, kct, vc_hm, kst, vs_hm, kwt, vw_hm, gates, bsz, seq)

    x1, h2 = _out_proj(y_ssm, y_att, x2, w_glu, b_glu, g_ssm_out, g_nsa_out, w_out, ga1, g_ffn_norm,
                       sh2, sc2, bsz, seq, tm)
    out = _conv_ffn(h2, x1, w_up, conv_w, conv_b, w_down, ga2, g_final, bsz, seq, tm, 512)
    return out.reshape(bsz, seq, d)


def kernel(x, c, w_ada, b_ada, g_mix_norm, w_in, lam_re, lam_im, log_dt, b_re, b_im, c_re, c_im, d_skip,
           w_glu, b_glu, pe_k, w1_k, w2_k, pe_v, w1_v, w2_v, g_ssm_out, g_nsa_out, w_out, g_ffn_norm,
           w_up, conv_w, conv_b, w_down, g_final):
    l = 0
    mod = _adaln_mod(c, w_ada[l], b_ada[l])
    return _block(x, mod, g_mix_norm[l], w_in[l],
                  (lam_re[l], lam_im[l], log_dt[l], b_re[l], b_im[l], c_re[l], c_im[l], d_skip[l]),
                  (pe_k[l], w1_k[l], w2_k[l]), (pe_v[l], w1_v[l], w2_v[l]),
                  g_ssm_out[l], g_nsa_out[l], w_glu[l], b_glu[l], w_out[l], g_ffn_norm[l],
                  w_up[l], conv_w[l], conv_b[l], w_down[l], g_final)
```

```python
import functools
import math

import jax
import jax.numpy as jnp
from jax import lax
from jax.experimental import pallas as pl
from jax.experimental.pallas import tpu as pltpu

F32 = jnp.float32
BF16 = jnp.bfloat16
HIGHEST = lax.Precision.HIGHEST

SSM_GROUP = 16
SSM_STATE = 64
N_HEADS = 16
N_KV_HEADS = 4
HEAD_DIM = 64
GQA = N_HEADS // N_KV_HEADS
KV_DIM = N_KV_HEADS * HEAD_DIM
ROT_DIM = HEAD_DIM // 4
ROPE_THETA = 500000.0
CMP_BLOCK = 32
CMP_STRIDE = 16
CMP_HIDDEN = 2 * HEAD_DIM
SEL_BLOCK = 64
N_SELECT = 16
WINDOW = 512
CONV_WIDTH = 3
NORM_EPS = 1e-6
NEG_BIG = -1e30
LOG2_E = math.log2(math.e)

LANES = 128
S5_CHUNK = 16
ATT_TILE = 512
ATT_SUB_ROWS = 256
VMEM_LIMIT = 56 * 2**20


def _params(sem, vmem=VMEM_LIMIT):
    return pltpu.CompilerParams(dimension_semantics=sem, vmem_limit_bytes=vmem)


def _const_spec(shape):
    n = len(shape)
    return pl.BlockSpec(shape, lambda *_: (0,) * n)


def _rms(x, g):
    return x * lax.rsqrt(jnp.mean(x * x, axis=-1, keepdims=True) + NORM_EPS) * g


def _dot(a, b):
    return jnp.dot(a, b, preferred_element_type=F32)


def _dot_nt(a, b, precision=None):
    return lax.dot_general(a, b, (((1,), (1,)), ((), ())), precision=precision,
                           preferred_element_type=F32)


def _mod_kernel(c_ref, w_ref, b_ref, o_ref):
    c = c_ref[...]
    sc = c * jax.nn.sigmoid(c)
    o_ref[...] = jnp.dot(sc, w_ref[...], preferred_element_type=F32, precision=HIGHEST) + b_ref[...]


def _adaln_mod(c, w_ada, b_ada):
    bsz, d = c.shape
    n = w_ada.shape[1]
    tn = n // 8
    cp = jnp.zeros((8, d), F32).at[:bsz].set(c)
    out = pl.pallas_call(
        _mod_kernel,
        grid=(n // tn,),
        in_specs=[pl.BlockSpec((8, d), lambda j: (0, 0)),
                  pl.BlockSpec((d, tn), lambda j: (0, j)),
                  pl.BlockSpec((1, tn), lambda j: (0, j))],
        out_specs=pl.BlockSpec((8, tn), lambda j: (0, j)),
        out_shape=jax.ShapeDtypeStruct((8, n), F32),
        compiler_params=_params(("arbitrary",)),
        name="adaln_mod",
    )(cp, w_ada, b_ada.reshape(1, n))
    return out[:bsz]


def _rope_kernel(c_ref, sm_ref, sp_ref, *, tr):
    i = pl.program_id(0)
    pos = (i * tr + lax.broadcasted_iota(jnp.int32, (tr, LANES), 0)).astype(F32)
    lane = lax.broadcasted_iota(jnp.int32, (tr, LANES), 1)
    d = lane & (HEAD_DIM - 1)
    half = ROT_DIM // 2
    fi = d & (half - 1)
    inv = jnp.zeros((tr, LANES), F32)
    for k in range(half):
        inv = jnp.where(fi == k, ROPE_THETA ** (-k / half), inv)
    ang = pos * inv
    cs = jnp.cos(ang)
    sn = jnp.sin(ang)
    c_ref[...] = jnp.where(d < ROT_DIM, cs, 1.0)
    sm_ref[...] = jnp.where(d < half, -sn, 0.0)
    sp_ref[...] = jnp.where((d >= half) & (d < ROT_DIM), sn, 0.0)


def _rope_tables(seq):
    tr = min(seq, 1024)
    sds = jax.ShapeDtypeStruct((seq, LANES), F32)
    spec = pl.BlockSpec((tr, LANES), lambda i: (i, 0))
    return pl.pallas_call(
        functools.partial(_rope_kernel, tr=tr),
        grid=(seq // tr,),
        in_specs=[],
        out_specs=[spec, spec, spec],
        out_shape=[sds, sds, sds],
        compiler_params=_params(("arbitrary",)),
        name="rope_tables",
    )()


def _proj_kernel(x_ref, g_ref, sh_ref, sc_ref, wu_ref, wq_ref, wkv_ref, wg_ref, rc_ref, rm_ref, rp_ref,
                 u_ref, q_ref, kc_ref, vc_ref, kst_ref, vs_ref, kwt_ref, vw_ref, gt_ref, usc):
    x = x_ref[...]
    h = _rms(x, g_ref[...]) * (1.0 + sc_ref[0]) + sh_ref[0]
    hb = h.astype(BF16)
    rc, rm, rp = rc_ref[...], rm_ref[...], rp_ref[...]

    def rope(s):
        return s * rc + pltpu.roll(s, LANES - ROT_DIM // 2, 1) * rm + pltpu.roll(s, ROT_DIM // 2, 1) * rp

    def roped(a):
        return jnp.concatenate([rope(a[:, :LANES]), rope(a[:, LANES:])], axis=1)

    u = _dot(hb, wu_ref[...])
    nchunk = x.shape[0] // S5_CHUNK
    for j in range(u_ref.shape[1]):
        usc[j] = u[:, j * LANES:(j + 1) * LANES]
        for k in range(S5_CHUNK):
            u_ref[k, j] = usc[j, pl.ds(k, nchunk, stride=S5_CHUNK), :].astype(BF16)

    q = _dot(hb, wq_ref[...])
    scale = HEAD_DIM ** -0.5 * LOG2_E
    for j in range(N_HEADS // 2):
        s = rope(q[:, j * LANES:(j + 1) * LANES]) * scale
        q_ref[0, 2 * j] = s[:, :HEAD_DIM].astype(BF16)
        q_ref[0, 2 * j + 1] = s[:, HEAD_DIM:].astype(BF16)

    kv = _dot(hb, wkv_ref[...])

    def seg(i):
        return kv[:, i * KV_DIM:(i + 1) * KV_DIM]

    kc_ref[...] = roped(seg(0)).astype(BF16)
    vc_ref[...] = seg(1).astype(BF16)
    kst_ref[0] = roped(seg(2)).T.astype(BF16)
    vs = seg(3)
    kwt_ref[0] = roped(seg(4)).T.astype(BF16)
    vw = seg(5)
    ones_col = jnp.where(lax.broadcasted_iota(jnp.int32, (x.shape[0], LANES - HEAD_DIM), 1) == 0, 1.0, 0.0)
    for hh in range(N_KV_HEADS):
        sl = slice(hh * HEAD_DIM, (hh + 1) * HEAD_DIM)
        vs_ref[0, hh] = jnp.concatenate([vs[:, sl], ones_col], axis=1).astype(BF16)
        vw_ref[0, hh] = jnp.concatenate([vw[:, sl], ones_col], axis=1).astype(BF16)

    sg = jax.nn.sigmoid(_dot(hb, wg_ref[...]))
    gt_ref[0] = sg
    for hh in range(1, N_KV_HEADS):
        gt_ref[hh] = pltpu.roll(sg, LANES - 3 * GQA * hh, 1)


def _input_proj(x2, g, sh, sc, w_in, rope_tabs, bsz, seq, tm):
    t, d = x2.shape
    d_ssm = d // 2
    d_att = d - d_ssm
    nt = seq // tm
    w_in = w_in.astype(BF16)
    o = 0
    wu = w_in[:, o:o + d_ssm]; o += d_ssm
    wq = w_in[:, o:o + d_att]; o += d_att
    wkv = w_in[:, o:o + 6 * KV_DIM]; o += 6 * KV_DIM
    wg = jnp.zeros((d, LANES), BF16).at[:, :3 * N_HEADS].set(w_in[:, o:])
    row = lambda i: (i, 0)
    bat = lambda i: (i // nt, 0, 0)
    tab = lambda i: (i % nt, 0)
    hm = lambda i: (i // nt, 0, i % nt, 0)
    tr = lambda i: (i // nt, 0, i % nt)
    in_specs = [pl.BlockSpec((tm, d), row), _const_spec((1, d)),
                pl.BlockSpec((1, 1, d), bat), pl.BlockSpec((1, 1, d), bat),
                _const_spec(wu.shape), _const_spec(wq.shape), _const_spec(wkv.shape), _const_spec(wg.shape),
                pl.BlockSpec((tm, LANES), tab), pl.BlockSpec((tm, LANES), tab), pl.BlockSpec((tm, LANES), tab)]
    out_shape = [jax.ShapeDtypeStruct((S5_CHUNK, d_ssm // LANES, t // S5_CHUNK, LANES), BF16),
                 jax.ShapeDtypeStruct((bsz, N_HEADS, seq, HEAD_DIM), BF16),
                 jax.ShapeDtypeStruct((t, KV_DIM), BF16),
                 jax.ShapeDtypeStruct((t, KV_DIM), BF16),
                 jax.ShapeDtypeStruct((bsz, KV_DIM, seq), BF16),
                 jax.ShapeDtypeStruct((bsz, N_KV_HEADS, seq, LANES), BF16),
                 jax.ShapeDtypeStruct((bsz, KV_DIM, seq), BF16),
                 jax.ShapeDtypeStruct((bsz, N_KV_HEADS, seq, LANES), BF16),
                 jax.ShapeDtypeStruct((N_KV_HEADS, t, LANES), F32)]
    out_specs = [pl.BlockSpec((S5_CHUNK, d_ssm // LANES, tm // S5_CHUNK, LANES), lambda i: (0, 0, i, 0)),
                 pl.BlockSpec((1, N_HEADS, tm, HEAD_DIM), hm),
                 pl.BlockSpec((tm, KV_DIM), row),
                 pl.BlockSpec((tm, KV_DIM), row),
                 pl.BlockSpec((1, KV_DIM, tm), tr),
                 pl.BlockSpec((1, N_KV_HEADS, tm, LANES), hm),
                 pl.BlockSpec((1, KV_DIM, tm), tr),
                 pl.BlockSpec((1, N_KV_HEADS, tm, LANES), hm),
                 pl.BlockSpec((N_KV_HEADS, tm, LANES), lambda i: (0, i, 0))]
    return pl.pallas_call(
        _proj_kernel,
        grid=(t // tm,),
        in_specs=in_specs, out_specs=out_specs, out_shape=out_shape,
        scratch_shapes=[pltpu.VMEM((d_ssm // LANES, tm, LANES), F32)],
        compiler_params=_params(("arbitrary",)),
        name="input_proj",
    )(x2, g.reshape(1, d), sh.reshape(bsz, 1, d), sc.reshape(bsz, 1, d), wu, wq, wkv, wg, *rope_tabs)


def _s5prep_kernel(lr_ref, li_ref, ldt_ref, btr_ref, bti_ref, cr_ref, ci_ref, dsk_ref,
                   m_ref, bzr_ref, bzi_ref, cyr_ref, cyi_ref, ar_ref, ai_ref):
    L, H, P = S5_CHUNK, SSM_GROUP, SSM_STATE
    lr = lr_ref[0]
    li = li_ref[0]
    dt = jnp.exp(ldt_ref[0])
    mag = jnp.exp(lr * dt)
    ab_re = mag * jnp.cos(li * dt)
    ab_im = mag * jnp.sin(li * dt)
    den = lr * lr + li * li
    num_re = ab_re - 1.0
    f_re = (num_re * lr + ab_im * li) / den
    f_im = (ab_im * lr - num_re * li) / den
    btr = btr_ref[0]
    bti = bti_ref[0]
    bb_re = f_re * btr - f_im * bti
    bb_im = f_re * bti + f_im * btr
    c_re = cr_ref[0]
    c_im = ci_ref[0]

    r = lax.broadcasted_iota(jnp.int32, (4 * L, 1), 0)
    k = r & (L - 1)
    blk = r >> 4
    e = jnp.where(blk == 0, -k, jnp.where(blk == 1, k, jnp.where(blk == 2, L - 1 - k, k + 1))).astype(F32)
    pm = jnp.exp(lr * dt * e)
    ph = li * dt * e
    pr = pm * jnp.cos(ph)
    pi = pm * jnp.sin(ph)

    def rows(i, sm_r, sm_i):
        a_r = jnp.broadcast_to(pr[i * L:(i + 1) * L][:, None, :], (L, H, P)).reshape(L * H, P)
        a_i = jnp.broadcast_to(pi[i * L:(i + 1) * L][:, None, :], (L, H, P)).reshape(L * H, P)
        s_r = jnp.broadcast_to(sm_r[None], (L, H, P)).reshape(L * H, P)
        s_i = jnp.broadcast_to(sm_i[None], (L, H, P)).reshape(L * H, P)
        return a_r * s_r - a_i * s_i, a_r * s_i + a_i * s_r

    x_re, x_im = rows(0, bb_re, bb_im)
    y_re, y_im = rows(1, c_re, c_im)
    z_re, z_im = rows(2, bb_re, bb_im)
    w_re, w_im = rows(3, c_re, c_im)
    mm = _dot_nt(x_re, y_re, HIGHEST) - _dot_nt(x_im, y_im, HIGHEST)
    ri = lax.broadcasted_iota(jnp.int32, (L * H, L * H), 0)
    ci = lax.broadcasted_iota(jnp.int32, (L * H, L * H), 1)
    mm = jnp.where((ci >> 4) >= (ri >> 4), mm, 0.0) + jnp.where(ri == ci, dsk_ref[0], 0.0)
    m_ref[0] = mm.astype(BF16)
    bzr_ref[0] = z_re.astype(BF16)
    bzi_ref[0] = z_im.astype(BF16)
    cyr_ref[0] = w_re.astype(BF16)
    cyi_ref[0] = (-w_im).astype(BF16)
    m16 = jnp.exp(lr * dt * L)
    ar_ref[0] = m16 * jnp.cos(li * dt * L)
    ai_ref[0] = m16 * jnp.sin(li * dt * L)


def _s5_prep(lam_re, lam_im, log_dt, b_re, b_im, c_re, c_im, d_skip):
    g, p, h = b_re.shape
    lh = S5_CHUNK * h
    g3 = lambda i: (i, 0, 0)
    ins = [lam_re.reshape(g, 1, p), lam_im.reshape(g, 1, p),
           jnp.broadcast_to(log_dt.reshape(g, 1, 1), (g, 1, p)),
           b_re.transpose(0, 2, 1), b_im.transpose(0, 2, 1), c_re, c_im,
           jnp.tile(d_skip, (1, S5_CHUNK)).reshape(g, 1, lh)]
    in_specs = [pl.BlockSpec((1,) + a.shape[1:], g3) for a in ins]
    out_shape = [jax.ShapeDtypeStruct((g, lh, lh), BF16)] + \
                [jax.ShapeDtypeStruct((g, lh, p), BF16)] * 4 + \
                [jax.ShapeDtypeStruct((g, 1, p), F32)] * 2
    out_specs = [pl.BlockSpec((1,) + s.shape[1:], g3) for s in out_shape]
    return pl.pallas_call(
        _s5prep_kernel, grid=(g,), in_specs=in_specs, out_specs=out_specs, out_shape=out_shape,
        compiler_params=_params(("arbitrary",)), name="s5_prep",
    )(*ins)


def _s5_kernel(u_ref, m_ref, bzr_ref, bzi_ref, cyr_ref, cyi_ref, ar_ref, ai_ref, y_ref,
               zr_sc, zi_sc, xr_sc, xi_sc, *, nch):
    L = S5_CHUNK
    xcat = jnp.concatenate([u_ref[k, 0] for k in range(L)], axis=1)
    zr_sc[...] = _dot(xcat, bzr_ref[0])
    zi_sc[...] = _dot(xcat, bzi_ref[0])
    a_r = ar_ref[0]
    a_i = ai_ref[0]

    def body(c, carry):
        x_r, x_i = carry
        xr_sc[pl.ds(c, 1), :] = x_r
        xi_sc[pl.ds(c, 1), :] = x_i
        z_r = zr_sc[pl.ds(c, 1), :]
        z_i = zi_sc[pl.ds(c, 1), :]
        return a_r * x_r - a_i * x_i + z_r, a_r * x_i + a_i * x_r + z_i

    zero = jnp.zeros((1, zr_sc.shape[1]), F32)
    lax.fori_loop(0, nch, body, (zero, zero))
    x_r = xr_sc[...].astype(BF16)
    x_i = xi_sc[...].astype(BF16)
    tw = 2 * LANES
    for n in range(L // 2):
        kk = tw * (n + 1)
        cols = slice(n * tw, (n + 1) * tw)
        y = (_dot(xcat[:, :kk], m_ref[0, :kk, cols]) + _dot(x_r, cyr_ref[0, :, cols])
             + _dot(x_i, cyi_ref[0, :, cols]))
        y_ref[2 * n, 0] = y[:, :LANES].astype(BF16)
        y_ref[2 * n + 1, 0] = y[:, LANES:].astype(BF16)


def _s5_block_kernel(m_ref, bzr_ref, bzi_ref, cyr_ref, cyi_ref, m2_ref, bzr2_ref, bzi2_ref, cyr2_ref, cyi2_ref):
    L, H, P = S5_CHUNK, SSM_GROUP, SSM_STATE
    gb = LANES // H
    w = L * LANES
    r = lax.broadcasted_iota(jnp.int32, (L * H, w), 0)
    q = lax.broadcasted_iota(jnp.int32, (L * H, w), 1)
    same_tc = ((r >> 4) == (q >> 7)) & ((r & (H - 1)) == (q & (H - 1)))
    q_grp = (q >> 4) & (gb - 1)
    sr = lax.broadcasted_iota(jnp.int32, (P, gb * P), 0)
    sq = lax.broadcasted_iota(jnp.int32, (P, gb * P), 1)
    eye_p = jnp.where(lax.broadcasted_iota(jnp.int32, (P, P), 0) == lax.broadcasted_iota(jnp.int32, (P, P), 1),
                      1.0, 0.0).astype(BF16)
    for g in range(gb):
        spread_c = jnp.where(same_tc & (q_grp == g), 1.0, 0.0).astype(BF16)
        spread_s = jnp.where(sq - g * P == sr, 1.0, 0.0).astype(BF16)
        xm = _dot(m_ref[g], spread_c).astype(BF16)
        xr = _dot(bzr_ref[g], spread_s).astype(BF16)
        xi = _dot(bzi_ref[g], spread_s).astype(BF16)
        for k in range(L):
            rows = slice(k * LANES + g * H, k * LANES + (g + 1) * H)
            m2_ref[0, rows, :] = xm[k * H:(k + 1) * H, :]
            bzr2_ref[0, rows, :] = xr[k * H:(k + 1) * H, :]
            bzi2_ref[0, rows, :] = xi[k * H:(k + 1) * H, :]
        for src, dst in ((cyr_ref, cyr2_ref), (cyi_ref, cyi2_ref)):
            cy_t = _dot_nt(eye_p, src[g]).astype(BF16)
            dst[0, g * P:(g + 1) * P, :] = _dot(cy_t, spread_c).astype(BF16)


def _s5_block_operators(ops, gb):
    m, bzr, bzi, cyr, cyi, a_r, a_i = ops
    g, lh, p = bzr.shape
    nb = g // gb
    w = S5_CHUNK * LANES
    sw = gb * p
    g3 = lambda j: (j, 0, 0)
    out_shape = [jax.ShapeDtypeStruct((nb, w, w), BF16), jax.ShapeDtypeStruct((nb, w, sw), BF16),
                 jax.ShapeDtypeStruct((nb, w, sw), BF16), jax.ShapeDtypeStruct((nb, sw, w), BF16),
                 jax.ShapeDtypeStruct((nb, sw, w), BF16)]
    m2, bzr2, bzi2, cyr2, cyi2 = pl.pallas_call(
        _s5_block_kernel,
        grid=(nb,),
        in_specs=[pl.BlockSpec((gb, lh, lh), g3)] + [pl.BlockSpec((gb, lh, p), g3)] * 4,
        out_specs=[pl.BlockSpec((1,) + s.shape[1:], g3) for s in out_shape],
        out_shape=out_shape,
        compiler_params=_params(("arbitrary",)),
        name="s5_block_ops",
    )(m, bzr, bzi, cyr, cyi)
    return m2, bzr2, bzi2, cyr2, cyi2, a_r.reshape(nb, 1, sw), a_i.reshape(nb, 1, sw)


def _s5_scan(u_k, ops, bsz):
    L, nb, rows, _ = u_k.shape
    gb = LANES // SSM_GROUP
    m2, bzr, bzi, cyr, cyi, a_r, a_i = _s5_block_operators(ops, gb)
    nch = rows // bsz
    w = L * LANES
    sw = gb * SSM_STATE
    op3 = lambda j, b: (j, 0, 0)
    xspec = pl.BlockSpec((L, 1, nch, LANES), lambda j, b: (0, j, b, 0))
    in_specs = [xspec, pl.BlockSpec((1, w, w), op3),
                pl.BlockSpec((1, w, sw), op3), pl.BlockSpec((1, w, sw), op3),
                pl.BlockSpec((1, sw, w), op3), pl.BlockSpec((1, sw, w), op3),
                pl.BlockSpec((1, 1, sw), op3), pl.BlockSpec((1, 1, sw), op3)]
    return pl.pallas_call(
        functools.partial(_s5_kernel, nch=nch),
        grid=(nb, bsz),
        in_specs=in_specs,
        out_specs=xspec,
        out_shape=jax.ShapeDtypeStruct(u_k.shape, BF16),
        scratch_shapes=[pltpu.VMEM((nch, sw), F32)] * 4,
        compiler_params=_params(("arbitrary", "arbitrary")),
        name="s5_scan",
    )(u_k, m2, bzr, bzi, cyr, cyi, a_r, a_i)


def _cmp_kernel(tok_ref, w1x_ref, pe_ref, w1_ref, w2x_ref, o_ref, *, transpose_out):
    nh = N_KV_HEADS * CMP_HIDDEN
    g = _dot(tok_ref[...], w1x_ref[...])
    rows = g.shape[0]
    bias = jnp.dot(jnp.broadcast_to(pe_ref[...], (8, pe_ref.shape[1])), w1_ref[...],
                   preferred_element_type=F32, precision=HIGHEST)[0:1]
    bias = jnp.concatenate([bias] * N_KV_HEADS, axis=1)
    hid = g[:, :nh] + pltpu.roll(g[:, nh:], rows - 1, 0) + bias
    out = _dot(jax.nn.gelu(hid).astype(BF16), w2x_ref[...])
    if transpose_out:
        o_ref[0] = out.T.astype(BF16)
    else:
        for hh in range(N_KV_HEADS):
            o_ref[0, hh] = out[:, hh * HEAD_DIM:(hh + 1) * HEAD_DIM].astype(BF16)


def _compress(tok, pe, w1, w2, bsz, seq, transpose_out):
    ncp = seq // CMP_STRIDE
    half = CMP_STRIDE * HEAD_DIM
    eye = jnp.eye(N_KV_HEADS, dtype=F32)
    w1r = w1.reshape(2, CMP_STRIDE, HEAD_DIM, CMP_HIDDEN)
    w1x = jnp.einsum('zkdj,hg->khdzgj', w1r, eye).reshape(CMP_STRIDE * KV_DIM, 2 * N_KV_HEADS * CMP_HIDDEN)
    w2x = jnp.einsum('jd,hg->hjgd', w2, eye).reshape(N_KV_HEADS * CMP_HIDDEN, KV_DIM)
    tok2 = tok.reshape(bsz * ncp, CMP_STRIDE * KV_DIM)
    if transpose_out:
        out_shape = jax.ShapeDtypeStruct((bsz, KV_DIM, ncp), BF16)
        out_spec = pl.BlockSpec((1, KV_DIM, ncp), lambda b: (b, 0, 0))
    else:
        out_shape = jax.ShapeDtypeStruct((bsz, N_KV_HEADS, ncp, HEAD_DIM), BF16)
        out_spec = pl.BlockSpec((1, N_KV_HEADS, ncp, HEAD_DIM), lambda b: (b, 0, 0, 0))
    return pl.pallas_call(
        functools.partial(_cmp_kernel, transpose_out=transpose_out),
        grid=(bsz,),
        in_specs=[pl.BlockSpec((ncp, CMP_STRIDE * KV_DIM), lambda b: (b, 0)),
                  _const_spec(w1x.shape), _const_spec((1, 2 * half)), _const_spec(w1.shape),
                  _const_spec(w2x.shape)],
        out_specs=out_spec, out_shape=out_shape,
        compiler_params=_params(("arbitrary",)),
        name="compress_k" if transpose_out else "compress_v",
    )(tok2, w1x.astype(BF16), pe.reshape(1, 2 * half), w1, w2x.astype(BF16))


def _nsa_kernel(q_ref, kct_ref, vc_ref, kst_ref, vs_ref, kwt_ref, vw_ref, gt_ref, amat_ref, o_ref,
                rhs_sc, lhs_sc, m_sc, st_sc, mw_sc, stw_sc, *, tq, seq):
    nblk = seq // SEL_BLOCK
    ncp = seq // CMP_STRIDE
    rows = GQA * tq
    nslab = tq // LANES
    sub = min(ATT_SUB_ROWS, tq)
    qi = pl.program_id(2)
    t0 = qi * tq

    @pl.when(qi == 0)
    def _():
        rhs_sc[0:HEAD_DIM, :] = kst_ref[0]
        rhs_sc[HEAD_DIM:LANES, :] = jnp.zeros((LANES - HEAD_DIM, seq), BF16)
        cw = min(seq, 1024)

        def fill(c, _):
            col = c * cw + lax.broadcasted_iota(jnp.int32, (nblk, cw), 1)
            blk = lax.broadcasted_iota(jnp.int32, (nblk, cw), 0)
            onehot = jnp.where((col >> 6) == blk, 1.0, 0.0).astype(BF16)
            rhs_sc[LANES:LANES + nblk, pl.ds(pl.multiple_of(c * cw, cw), cw)] = onehot
            return 0

        lax.fori_loop(0, seq // cw, fill, 0)

    t_q = t0 + lax.broadcasted_iota(jnp.int32, (tq, 1), 0)

    def reset(state):
        state[0][...] = jnp.full((rows, LANES), NEG_BIG, F32)
        state[1][...] = jnp.zeros((rows, LANES), F32)

    def online(state, g, s, v_aug, h=None):
        m_ref, st_ref = state
        sl = slice(g * tq, (g + 1) * tq) if h is None else slice(g * tq + h * sub, g * tq + (h + 1) * sub)
        m_old = m_ref[sl, :]
        mx = s[:, 0:LANES]
        for j in range(1, nslab):
            mx = jnp.maximum(mx, s[:, j * LANES:(j + 1) * LANES])
        m_new = jnp.maximum(m_old, jnp.max(mx, axis=-1, keepdims=True))
        alpha = jnp.exp2(m_old - m_new)
        p = jnp.concatenate([jnp.exp2(s[:, j * LANES:(j + 1) * LANES] - m_new) for j in range(nslab)], axis=1)
        st_ref[sl, :] = alpha * st_ref[sl, :] + _dot(p.astype(BF16), v_aug)
        m_ref[sl, :] = m_new

    def result(state, g):
        st = state[1][g * tq:(g + 1) * tq, :]
        return st[:, 0:HEAD_DIM] * (1.0 / st[:, HEAD_DIM:HEAD_DIM + 1])

    def kpos(kj):
        return kj * tq + lax.broadcasted_iota(jnp.int32, (1, tq), 1)

    win_st = (mw_sc, stw_sc)
    reset(win_st)
    for w in range(WINDOW // tq + 1):
        kj = qi - WINDOW // tq + w
        off = pl.multiple_of(jnp.maximum(kj, 0) * tq, tq)
        kt = kwt_ref[0, :, pl.ds(off, tq)]
        v_aug = vw_ref[0, 0, pl.ds(off, tq), :]
        kp = kpos(kj)
        if w == WINDOW // tq:
            mask = kp <= t_q
        elif w == 0:
            mask = (kp > t_q - WINDOW) & (kp >= 0)
        else:
            mask = jnp.broadcast_to(kp >= 0, (tq, tq))
        for g in range(GQA):
            online(win_st, g, jnp.where(mask, _dot(q_ref[0, g], kt), NEG_BIG), v_aug)

    n_col = lax.broadcasted_iota(jnp.int32, (1, ncp), 1)
    valid = (n_col * CMP_STRIDE + (CMP_BLOCK - 1)) <= t_q
    row_live = t_q >= CMP_BLOCK - 1
    kct = kct_ref[0]
    vc = vc_ref[0, 0]
    o_cmp = []
    psum = None
    for g in range(GQA):
        sm = jnp.where(valid, _dot(q_ref[0, g], kct), NEG_BIG)
        e = jnp.exp2(sm - jnp.max(sm, axis=-1, keepdims=True))
        p = e * jnp.where(row_live, 1.0 / jnp.sum(e, axis=-1, keepdims=True), 0.0)
        o_cmp.append(_dot(p.astype(BF16), vc))
        psum = p if g == 0 else psum + p

    amat = amat_ref[...]
    p_hi = psum.astype(BF16)
    r1 = psum - p_hi.astype(F32)
    p_mid = r1.astype(BF16)
    p_lo = (r1 - p_mid.astype(F32)).astype(BF16)
    imp = _dot(p_hi, amat) + _dot(p_mid, amat) + _dot(p_lo, amat)
    blk = lax.broadcasted_iota(jnp.int32, (nblk, 1), 0)
    t_l = t0 + lax.broadcasted_iota(jnp.int32, (1, tq), 1)
    cur = t_l >> 6
    forced = (blk == 0) | (blk == cur) | (blk == cur - 1)
    impm = jnp.where(forced, -jnp.inf, jnp.where(blk * SEL_BLOCK <= t_l, imp.T, NEG_BIG))
    impm = impm.reshape(nblk // 8, 8, tq)
    n_forced = 3
    blk_f = (lax.broadcasted_iota(jnp.int32, (nblk // 8, 8, tq), 0) * 8
             + lax.broadcasted_iota(jnp.int32, (nblk // 8, 8, tq), 1)).astype(F32)

    def over_blocks(x, op):
        r = x[0]
        for a in range(1, nblk // 8):
            r = op(r, x[a])
        for sh in (4, 2, 1):
            r = op(r, pltpu.roll(r, sh, 0))
        return r[None]

    selneg = jnp.where(forced, 0.0, NEG_BIG).reshape(nblk // 8, 8, tq)
    for _ in range(max(min(N_SELECT, nblk) - n_forced, 0)):
        best = over_blocks(impm, jnp.maximum)
        idx = over_blocks(jnp.where(impm == best, blk_f, float(nblk)), jnp.minimum)
        pick = blk_f == idx
        selneg = jnp.where(pick, 0.0, selneg)
        impm = jnp.where(pick, -jnp.inf, impm)
    selneg = selneg.reshape(nblk, tq).T.astype(BF16)

    for g in range(GQA):
        sl = slice(g * tq, (g + 1) * tq)
        lhs_sc[sl, 0:HEAD_DIM] = q_ref[0, g]
        lhs_sc[sl, HEAD_DIM:LANES] = jnp.zeros((tq, LANES - HEAD_DIM), BF16)
        lhs_sc[sl, LANES:LANES + nblk] = selneg

    sel_st = (m_sc, st_sc)
    reset(sel_st)

    def sel_scores(kj):
        rhs = rhs_sc[:, pl.ds(pl.multiple_of(kj * tq, tq), tq)]
        return tuple(_dot(lhs_sc[g * tq + h * sub:g * tq + (h + 1) * sub, :], rhs)
                     for g in range(GQA) for h in range(tq // sub))

    def sel_update(kj, scores, mask):
        v_aug = vs_ref[0, 0, pl.ds(pl.multiple_of(kj * tq, tq), tq), :]
        for g in range(GQA):
            for h in range(tq // sub):
                sc = scores[g * (tq // sub) + h]
                if mask is not None:
                    sc = jnp.where(mask[h * sub:(h + 1) * sub], sc, NEG_BIG)
                online(sel_st, g, sc, v_aug, h)

    def sel_step(kj, scores):
        nxt = sel_scores(kj + 1)
        sel_update(kj, scores, None)
        return nxt

    scores = lax.fori_loop(0, qi, sel_step, sel_scores(0))
    sel_update(qi, scores, kpos(qi) <= t_q)

    gt = gt_ref[0]
    for g in range(GQA):
        o = (gt[:, 3 * g:3 * g + 1] * o_cmp[g] + gt[:, 3 * g + 1:3 * g + 2] * result(sel_st, g)
             + gt[:, 3 * g + 2:3 * g + 3] * result(win_st, g))
        o_ref[:, g * HEAD_DIM:(g + 1) * HEAD_DIM] = o


def _importance_matrix(seq):
    ncp = seq // CMP_STRIDE
    nblk = seq // SEL_BLOCK
    ratio = SEL_BLOCK // CMP_STRIDE
    n = jnp.arange(ncp)[:, None]
    j = jnp.arange(nblk)[None, :]
    a = sum(((n == ratio * j + r).astype(F32) + (n == ratio * j + r + 1).astype(F32)) for r in range(ratio))
    return jnp.where(n < ncp - 1, a, 0.0).astype(BF16)


def _nsa_attention(q_hm, kct, vc_hm, kst, vs_hm, kwt, vw_hm, gates, bsz, seq):
    tq = min(ATT_TILE, seq)
    nq = seq // tq
    nblk = seq // SEL_BLOCK
    ncp = seq // CMP_STRIDE
    rows = GQA * tq
    amat = _importance_matrix(seq)
    head = lambda b, h, i: (b, h, 0, 0)
    headt = lambda b, h, i: (b, h, 0)
    in_specs = [pl.BlockSpec((1, GQA, tq, HEAD_DIM), lambda b, h, i: (b, h, i, 0)),
                pl.BlockSpec((1, HEAD_DIM, ncp), headt),
                pl.BlockSpec((1, 1, ncp, HEAD_DIM), head),
                pl.BlockSpec((1, HEAD_DIM, seq), headt),
                pl.BlockSpec((1, 1, seq, LANES), head),
                pl.BlockSpec((1, HEAD_DIM, seq), headt),
                pl.BlockSpec((1, 1, seq, LANES), head),
                pl.BlockSpec((1, tq, LANES), lambda b, h, i: (h, b * nq + i, 0)),
                _const_spec(amat.shape)]
    return pl.pallas_call(
        functools.partial(_nsa_kernel, tq=tq, seq=seq),
        grid=(bsz, N_KV_HEADS, nq),
        in_specs=in_specs,
        out_specs=pl.BlockSpec((tq, GQA * HEAD_DIM), lambda b, h, i: (b * nq + i, h)),
        out_shape=jax.ShapeDtypeStruct((bsz * seq, N_HEADS * HEAD_DIM), F32),
        scratch_shapes=[pltpu.VMEM((LANES + nblk, seq), BF16),
                        pltpu.VMEM((rows, LANES + nblk), BF16),
                        pltpu.VMEM((rows, LANES), F32), pltpu.VMEM((rows, LANES), F32),
                        pltpu.VMEM((rows, LANES), F32), pltpu.VMEM((rows, LANES), F32)],
        compiler_params=_params(("arbitrary", "arbitrary", "arbitrary")),
        name="nsa_attention",
    )(q_hm, kct, vc_hm, kst, vs_hm, kwt, vw_hm, gates, amat)


def _out_kernel(ys_ref, ya_ref, x_ref, wglu_ref, bglu_ref, gs_ref, ga_ref, woa_ref, wob_ref,
                ga1_ref, gf_ref, sh2_ref, sc2_ref, x1_ref, h2_ref, ysc):
    nchunk = ysc.shape[1] // S5_CHUNK
    for j in range(ysc.shape[0]):
        for k in range(S5_CHUNK):
            ysc[j, pl.ds(k, nchunk, stride=S5_CHUNK), :] = ys_ref[k, j].astype(F32)
    z = jax.nn.gelu(jnp.concatenate([ysc[j] for j in range(ysc.shape[0])], axis=1))
    glu = z * jax.nn.sigmoid(_dot(z.astype(BF16), wglu_ref[...]) + bglu_ref[...])
    n1 = _rms(glu, gs_ref[...]).astype(BF16)
    n2 = _rms(ya_ref[...], ga_ref[...]).astype(BF16)
    y = _dot(n1, woa_ref[...]) + _dot(n2, wob_ref[...])
    x1 = x_ref[...] + ga1_ref[0] * y
    x1_ref[...] = x1
    h2_ref[...] = (_rms(x1, gf_ref[...]) * (1.0 + sc2_ref[0]) + sh2_ref[0]).astype(BF16)


def _out_proj(y_ssm, y_att, x2, w_glu, b_glu, g_ssm, g_nsa, w_out, ga1, g_ffn, sh2, sc2, bsz, seq, tm):
    t, d = x2.shape
    d_att = y_att.shape[1]
    d_ssm = d - d_att
    nt = seq // tm
    row = lambda i: (i, 0)
    bat = lambda i: (i // nt, 0, 0)
    w_out = w_out.astype(BF16)
    in_specs = [pl.BlockSpec((S5_CHUNK, d_ssm // LANES, tm // S5_CHUNK, LANES), lambda i: (0, 0, i, 0)),
                pl.BlockSpec((tm, d_att), row), pl.BlockSpec((tm, d), row),
                _const_spec((d_ssm, d_ssm)), _const_spec((1, d_ssm)), _const_spec((1, d_ssm)),
                _const_spec((1, d_att)), _const_spec((d_ssm, d)), _const_spec((d_att, d)),
                pl.BlockSpec((1, 1, d), bat), _const_spec((1, d)),
                pl.BlockSpec((1, 1, d), bat), pl.BlockSpec((1, 1, d), bat)]
    return pl.pallas_call(
        _out_kernel,
        grid=(t // tm,),
        in_specs=in_specs,
        out_specs=[pl.BlockSpec((tm, d), row), pl.BlockSpec((tm, d), row)],
        out_shape=[jax.ShapeDtypeStruct((t, d), F32), jax.ShapeDtypeStruct((t, d), BF16)],
        scratch_shapes=[pltpu.VMEM((d_ssm // LANES, tm, LANES), F32)],
        compiler_params=_params(("arbitrary",)),
        name="out_proj",
    )(y_ssm, y_att, x2, w_glu.astype(BF16), b_glu.reshape(1, d_ssm), g_ssm.reshape(1, d_ssm),
      g_nsa.reshape(1, d_att), w_out[:d_ssm], w_out[d_ssm:], ga1.reshape(bsz, 1, d), g_ffn.reshape(1, d),
      sh2.reshape(bsz, 1, d), sc2.reshape(bsz, 1, d))


FFN_HALO = 16
FFN_COL = 256


def _ffn_kernel(h_ref, halo_ref, x1_ref, wv_ref, wg_ref, cwv_ref, cwg_ref, cbv_ref, cbg_ref, wd_ref,
                ga2_ref, gfin_ref, o_ref, up_a, up_b, acc_sc, *, nt, nj, nsteps):
    s = pl.program_id(0)
    su = jnp.minimum(s, nsteps - 1)
    sd = jnp.maximum(s - 1, 0)
    iu = lax.div(su, nj)
    jd = lax.rem(sd, nj)

    @pl.when(s == 0)
    def _():
        up_a[...] = jnp.zeros(up_a.shape, F32)
        up_b[...] = jnp.zeros(up_b.shape, F32)
        acc_sc[...] = jnp.zeros(acc_sc.shape, F32)

    def step(up_prev, up_next):
        def conv(i, cw_ref, cb_ref, cols):
            up = up_prev[i, :, cols]
            cw = cw_ref[:, cols]
            c = (pltpu.roll(up, 2, 0) * cw[0:1] + pltpu.roll(up, 1, 0) * cw[1:2] + up * cw[2:3])
            return c[FFN_HALO:] + cb_ref[:, cols]

        halo = halo_ref[...]
        halo = jnp.where(lax.rem(iu, nt) == 0, jnp.zeros_like(halo), halo)
        hx = jnp.concatenate([halo, h_ref[...]], axis=0)

        part = None
        for c0 in range(0, wv_ref.shape[1], FFN_COL):
            cols = slice(c0, c0 + FFN_COL)
            up_next[0, :, cols] = _dot(hx, wv_ref[:, cols])
            val = conv(0, cwv_ref, cbv_ref, cols)
            gate = conv(1, cwg_ref, cbg_ref, cols)
            act = (gate * jax.nn.sigmoid(gate) * val).astype(BF16)
            up_next[1, :, cols] = _dot(hx, wg_ref[:, cols])
            d = _dot(act, wd_ref[cols, :])
            part = d if part is None else part + d
        acc_sc[...] = jnp.where(jd > 0, acc_sc[...], 0.0) + part

    odd = lax.rem(s, 2)

    @pl.when(odd == 0)
    def _():
        step(up_b, up_a)

    @pl.when(odd == 1)
    def _():
        step(up_a, up_b)

    @pl.when((s > 0) & (jd == nj - 1))
    def _():
        x2 = x1_ref[...] + ga2_ref[0] * acc_sc[...]
        o_ref[...] = _rms(x2, gfin_ref[...])


def _conv_ffn(h2, x1, w_up, conv_w, conv_b, w_down, ga2, g_final, bsz, seq, tm, tn):
    t, d = x1.shape
    dff = w_down.shape[0]
    nt = seq // tm
    nj = dff // tn
    nsteps = (t // tm) * nj
    w_up = w_up.astype(BF16)
    w_down = w_down.astype(BF16)
    cb = conv_b.reshape(1, 2 * dff)
    hb = tm // FFN_HALO
    iu = lambda s: lax.div(jnp.minimum(s, nsteps - 1), nj)
    ju = lambda s: lax.rem(jnp.minimum(s, nsteps - 1), nj)
    idn = lambda s: lax.div(jnp.maximum(s - 1, 0), nj)
    jdn = lambda s: lax.rem(jnp.maximum(s - 1, 0), nj)
    in_specs = [pl.BlockSpec((tm, d), lambda s: (iu(s), 0)),
                pl.BlockSpec((FFN_HALO, d), lambda s: (jnp.maximum(iu(s) * hb - 1, 0), 0)),
                pl.BlockSpec((tm, d), lambda s: (idn(s), 0)),
                pl.BlockSpec((d, tn), lambda s: (0, ju(s))),
                pl.BlockSpec((d, tn), lambda s: (0, nj + ju(s))),
                pl.BlockSpec((CONV_WIDTH, tn), lambda s: (0, jdn(s))),
                pl.BlockSpec((CONV_WIDTH, tn), lambda s: (0, nj + jdn(s))),
                pl.BlockSpec((1, tn), lambda s: (0, jdn(s))),
                pl.BlockSpec((1, tn), lambda s: (0, nj + jdn(s))),
                pl.BlockSpec((tn, d), lambda s: (jdn(s), 0)),
                pl.BlockSpec((1, 1, d), lambda s: (lax.div(idn(s), nt), 0, 0)),
                pl.BlockSpec((1, d), lambda s: (0, 0))]
    return pl.pallas_call(
        functools.partial(_ffn_kernel, nt=nt, nj=nj, nsteps=nsteps),
        grid=(nsteps + 1,),
        in_specs=in_specs,
        out_specs=pl.BlockSpec((tm, d), lambda s: (idn(s), 0)),
        out_shape=jax.ShapeDtypeStruct((t, d), F32),
        scratch_shapes=[pltpu.VMEM((2, FFN_HALO + tm, tn), F32), pltpu.VMEM((2, FFN_HALO + tm, tn), F32),
                        pltpu.VMEM((tm, d), F32)],
        compiler_params=_params(("arbitrary",)),
        name="conv_ffn",
    )(h2, h2, x1, w_up, w_up, conv_w, conv_w, cb, cb, w_down, ga2.reshape(bsz, 1, d), g_final.reshape(1, d))


def _block(x, mod, g_mix_norm, w_in, s5_params, cmp_k, cmp_v, g_ssm_out, g_nsa_out, w_glu, b_glu, w_out,
           g_ffn_norm, w_up, conv_w, conv_b, w_down, g_final):
    bsz, seq, d = x.shape
    t = bsz * seq
    d_ssm = d // 2
    x2 = x.reshape(t, d)
    sh1, sc1, ga1, sh2, sc2, ga2 = jnp.split(mod, 6, axis=-1)
    tm = min(512, seq)

    u, q_hm, kc_tok, vc_tok, kst, vs_hm, kwt, vw_hm, gates = _input_proj(
        x2, g_mix_norm, sh1, sc1, w_in, _rope_tables(seq), bsz, seq, tm)

    y_ssm = _s5_scan(u, _s5_prep(*s5_params), bsz)

    kct = _compress(kc_tok, *cmp_k, bsz, seq, True)
    vc_hm = _compress(vc_tok, *cmp_v, bsz, seq, False)
    y_att = _nsa_attention(q_hm, kct, vc_hm, kst, vs_hm, kwt, vw_hm, gates, bsz, seq)

    x1, h2 = _out_proj(y_ssm, y_att, x2, w_glu, b_glu, g_ssm_out, g_nsa_out, w_out, ga1, g_ffn_norm,
                       sh2, sc2, bsz, seq, tm)
    out = _conv_ffn(h2, x1, w_up, conv_w, conv_b, w_down, ga2, g_final, bsz, seq, tm, 512)
    return out.reshape(bsz, seq, d)


def kernel(x, c, w_ada, b_ada, g_mix_norm, w_in, lam_re, lam_im, log_dt, b_re, b_im, c_re, c_im, d_skip,
           w_glu, b_glu, pe_k, w1_k, w2_k, pe_v, w1_v, w2_v, g_ssm_out, g_nsa_out, w_out, g_ffn_norm,
           w_up, conv_w, conv_b, w_down, g_final):
    l = 0
    mod = _adaln_mod(c, w_ada[l], b_ada[l])
    return _block(x, mod, g_mix_norm[l], w_in[l],
                  (lam_re[l], lam_im[l], log_dt[l], b_re[l], b_im[l], c_re[l], c_im[l], d_skip[l]),
                  (pe_k[l], w1_k[l], w2_k[l]), (pe_v[l], w1_v[l], w2_v[l]),
                  g_ssm_out[l], g_nsa_out[l], w_glu[l], b_glu[l], w_out[l], g_ffn_norm[l],
                  w_up[l], conv_w[l], conv_b[l], w_down[l], g_final)
```

```python
import functools
import math

import jax
import jax.numpy as jnp
from jax import lax
from jax.experimental import pallas as pl
from jax.experimental.pallas import tpu as pltpu

F32 = jnp.float32
BF16 = jnp.bfloat16
HIGHEST = lax.Precision.HIGHEST

SSM_GROUP = 16
SSM_STATE = 64
N_HEADS = 16
N_KV_HEADS = 4
HEAD_DIM = 64
GQA = N_HEADS // N_KV_HEADS
KV_DIM = N_KV_HEADS * HEAD_DIM
ROT_DIM = HEAD_DIM // 4
ROPE_THETA = 500000.0
CMP_BLOCK = 32
CMP_STRIDE = 16
CMP_HIDDEN = 2 * HEAD_DIM
SEL_BLOCK = 64
N_SELECT = 16
WINDOW = 512
CONV_WIDTH = 3
NORM_EPS = 1e-6
NEG_BIG = -1e30
LOG2_E = math.log2(math.e)

LANES = 128
S5_CHUNK = 16
ATT_TILE = 512
ATT_SUB_ROWS = 256
VMEM_LIMIT = 56 * 2**20


def _params(sem, vmem=VMEM_LIMIT):
    return pltpu.CompilerParams(dimension_semantics=sem, vmem_limit_bytes=vmem)


def _const_spec(shape):
    n = len(shape)
    return pl.BlockSpec(shape, lambda *_: (0,) * n)


def _rms(x, g):
    return x * lax.rsqrt(jnp.mean(x * x, axis=-1, keepdims=True) + NORM_EPS) * g


def _dot(a, b):
    return jnp.dot(a, b, preferred_element_type=F32)


def _dot_nt(a, b, precision=None):
    return lax.dot_general(a, b, (((1,), (1,)), ((), ())), precision=precision,
                           preferred_element_type=F32)


def _mod_kernel(c_ref, w_ref, b_ref, o_ref):
    c = c_ref[...]
    sc = c * jax.nn.sigmoid(c)
    o_ref[...] = jnp.dot(sc, w_ref[...], preferred_element_type=F32, precision=HIGHEST) + b_ref[...]


def _adaln_mod(c, w_ada, b_ada):
    bsz, d = c.shape
    n = w_ada.shape[1]
    tn = n // 8
    cp = jnp.zeros((8, d), F32).at[:bsz].set(c)
    out = pl.pallas_call(
        _mod_kernel,
        grid=(n // tn,),
        in_specs=[pl.BlockSpec((8, d), lambda j: (0, 0)),
                  pl.BlockSpec((d, tn), lambda j: (0, j)),
                  pl.BlockSpec((1, tn), lambda j: (0, j))],
        out_specs=pl.BlockSpec((8, tn), lambda j: (0, j)),
        out_shape=jax.ShapeDtypeStruct((8, n), F32),
        compiler_params=_params(("arbitrary",)),
        name="adaln_mod",
    )(cp, w_ada, b_ada.reshape(1, n))
    return out[:bsz]


def _rope_kernel(c_ref, sm_ref, sp_ref, *, tr):
    i = pl.program_id(0)
    pos = (i * tr + lax.broadcasted_iota(jnp.int32, (tr, LANES), 0)).astype(F32)
    lane = lax.broadcasted_iota(jnp.int32, (tr, LANES), 1)
    d = lane & (HEAD_DIM - 1)
    half = ROT_DIM // 2
    fi = d & (half - 1)
    inv = jnp.zeros((tr, LANES), F32)
    for k in range(half):
        inv = jnp.where(fi == k, ROPE_THETA ** (-k / half), inv)
    ang = pos * inv
    cs = jnp.cos(ang)
    sn = jnp.sin(ang)
    c_ref[...] = jnp.where(d < ROT_DIM, cs, 1.0)
    sm_ref[...] = jnp.where(d < half, -sn, 0.0)
    sp_ref[...] = jnp.where((d >= half) & (d < ROT_DIM), sn, 0.0)


def _rope_tables(seq):
    tr = min(seq, 1024)
    sds = jax.ShapeDtypeStruct((seq, LANES), F32)
    spec = pl.BlockSpec((tr, LANES), lambda i: (i, 0))
    return pl.pallas_call(
        functools.partial(_rope_kernel, tr=tr),
        grid=(seq // tr,),
        in_specs=[],
        out_specs=[spec, spec, spec],
        out_shape=[sds, sds, sds],
        compiler_params=_params(("arbitrary",)),
        name="rope_tables",
    )()


def _proj_kernel(x_ref, g_ref, sh_ref, sc_ref, wu_ref, wq_ref, wkv_ref, wg_ref, rc_ref, rm_ref, rp_ref,
                 u_ref, q_ref, kc_ref, vc_ref, kst_ref, vs_ref, kwt_ref, vw_ref, gt_ref, usc):
    x = x_ref[...]
    h = _rms(x, g_ref[...]) * (1.0 + sc_ref[0]) + sh_ref[0]
    hb = h.astype(BF16)
    rc, rm, rp = rc_ref[...], rm_ref[...], rp_ref[...]

    def rope(s):
        return s * rc + pltpu.roll(s, LANES - ROT_DIM // 2, 1) * rm + pltpu.roll(s, ROT_DIM // 2, 1) * rp

    def roped(a):
        return jnp.concatenate([rope(a[:, :LANES]), rope(a[:, LANES:])], axis=1)

    u = _dot(hb, wu_ref[...])
    nchunk = x.shape[0] // S5_CHUNK
    for j in range(u_ref.shape[1]):
        usc[j] = u[:, j * LANES:(j + 1) * LANES]
        for k in range(S5_CHUNK):
            u_ref[k, j] = usc[j, pl.ds(k, nchunk, stride=S5_CHUNK), :].astype(BF16)

    q = _dot(hb, wq_ref[...])
    scale = HEAD_DIM ** -0.5 * LOG2_E
    for j in range(N_HEADS // 2):
        s = rope(q[:, j * LANES:(j + 1) * LANES]) * scale
        q_ref[0, 2 * j] = s[:, :HEAD_DIM].astype(BF16)
        q_ref[0, 2 * j + 1] = s[:, HEAD_DIM:].astype(BF16)

    kv = _dot(hb, wkv_ref[...])

    def seg(i):
        return kv[:, i * KV_DIM:(i + 1) * KV_DIM]

    kc_ref[...] = roped(seg(0)).astype(BF16)
    vc_ref[...] = seg(1).astype(BF16)
    kst_ref[0] = roped(seg(2)).T.astype(BF16)
    vs = seg(3)
    kwt_ref[0] = roped(seg(4)).T.astype(BF16)
    vw = seg(5)
    ones_col = jnp.where(lax.broadcasted_iota(jnp.int32, (x.shape[0], LANES - HEAD_DIM), 1) == 0, 1.0, 0.0)
    for hh in range(N_KV_HEADS):
        sl = slice(hh * HEAD_DIM, (hh + 1) * HEAD_DIM)
        vs_ref[0, hh] = jnp.concatenate([vs[:, sl], ones_col], axis=1).astype(BF16)
        vw_ref[0, hh] = jnp.concatenate([vw[:, sl], ones_col], axis=1).astype(BF16)

    sg = jax.nn.sigmoid(_dot(hb, wg_ref[...]))
    gt_ref[0] = sg
    for hh in range(1, N_KV_HEADS):
        gt_ref[hh] = pltpu.roll(sg, LANES - 3 * GQA * hh, 1)


def _input_proj(x2, g, sh, sc, w_in, rope_tabs, bsz, seq, tm):
    t, d = x2.shape
    d_ssm = d // 2
    d_att = d - d_ssm
    nt = seq // tm
    w_in = w_in.astype(BF16)
    o = 0
    wu = w_in[:, o:o + d_ssm]; o += d_ssm
    wq = w_in[:, o:o + d_att]; o += d_att
    wkv = w_in[:, o:o + 6 * KV_DIM]; o += 6 * KV_DIM
    wg = jnp.zeros((d, LANES), BF16).at[:, :3 * N_HEADS].set(w_in[:, o:])
    row = lambda i: (i, 0)
    bat = lambda i: (i // nt, 0, 0)
    tab = lambda i: (i % nt, 0)
    hm = lambda i: (i // nt, 0, i % nt, 0)
    tr = lambda i: (i // nt, 0, i % nt)
    in_specs = [pl.BlockSpec((tm, d), row), _const_spec((1, d)),
                pl.BlockSpec((1, 1, d), bat), pl.BlockSpec((1, 1, d), bat),
                _const_spec(wu.shape), _const_spec(wq.shape), _const_spec(wkv.shape), _const_spec(wg.shape),
                pl.BlockSpec((tm, LANES), tab), pl.BlockSpec((tm, LANES), tab), pl.BlockSpec((tm, LANES), tab)]
    out_shape = [jax.ShapeDtypeStruct((S5_CHUNK, d_ssm // LANES, t // S5_CHUNK, LANES), BF16),
                 jax.ShapeDtypeStruct((bsz, N_HEADS, seq, HEAD_DIM), BF16),
                 jax.ShapeDtypeStruct((t, KV_DIM), BF16),
                 jax.ShapeDtypeStruct((t, KV_DIM), BF16),
                 jax.ShapeDtypeStruct((bsz, KV_DIM, seq), BF16),
                 jax.ShapeDtypeStruct((bsz, N_KV_HEADS, seq, LANES), BF16),
                 jax.ShapeDtypeStruct((bsz, KV_DIM, seq), BF16),
                 jax.ShapeDtypeStruct((bsz, N_KV_HEADS, seq, LANES), BF16),
                 jax.ShapeDtypeStruct((N_KV_HEADS, t, LANES), F32)]
    out_specs = [pl.BlockSpec((S5_CHUNK, d_ssm // LANES, tm // S5_CHUNK, LANES), lambda i: (0, 0, i, 0)),
                 pl.BlockSpec((1, N_HEADS, tm, HEAD_DIM), hm),
                 pl.BlockSpec((tm, KV_DIM), row),
                 pl.BlockSpec((tm, KV_DIM), row),
                 pl.BlockSpec((1, KV_DIM, tm), tr),
                 pl.BlockSpec((1, N_KV_HEADS, tm, LANES), hm),
                 pl.BlockSpec((1, KV_DIM, tm), tr),
                 pl.BlockSpec((1, N_KV_HEADS, tm, LANES), hm),
                 pl.BlockSpec((N_KV_HEADS, tm, LANES), lambda i: (0, i, 0))]
    return pl.pallas_call(
        _proj_kernel,
        grid=(t // tm,),
        in_specs=in_specs, out_specs=out_specs, out_shape=out_shape,
        scratch_shapes=[pltpu.VMEM((d_ssm // LANES, tm, LANES), F32)],
        compiler_params=_params(("arbitrary",)),
        name="input_proj",
    )(x2, g.reshape(1, d), sh.reshape(bsz, 1, d), sc.reshape(bsz, 1, d), wu, wq, wkv, wg, *rope_tabs)


S5_LEVELS = (8, 4, 2, 1)


def _s5_exponents():
    L = S5_CHUNK
    k = jnp.arange(L)
    blocks = [jnp.zeros((L,), jnp.int32)]
    blocks += [jnp.maximum(s - k % (2 * s), 0) for s in S5_LEVELS]
    blocks += [jnp.maximum(k % (2 * s) - s, 0) for s in S5_LEVELS]
    blocks += [L - 1 - k, k + 1]
    return jnp.broadcast_to(jnp.concatenate(blocks).astype(F32)[:, None], (len(blocks) * L, SSM_STATE))


def _s5prep_kernel(lr_ref, li_ref, ldt_ref, btr_ref, bti_ref, cr_ref, ci_ref, dsk_ref, etab_ref,
                   m_ref, bzr_ref, bzi_ref, cyr_ref, cyi_ref, ar_ref, ai_ref):
    L, H, P = S5_CHUNK, SSM_GROUP, SSM_STATE
    lr = lr_ref[0]
    li = li_ref[0]
    dt = jnp.exp(ldt_ref[0])
    mag = jnp.exp(lr * dt)
    ab_re = mag * jnp.cos(li * dt)
    ab_im = mag * jnp.sin(li * dt)
    den = lr * lr + li * li
    num_re = ab_re - 1.0
    f_re = (num_re * lr + ab_im * li) / den
    f_im = (ab_im * lr - num_re * li) / den
    btr = btr_ref[0]
    bti = bti_ref[0]
    bb_re = f_re * btr - f_im * bti
    bb_im = f_re * bti + f_im * btr
    c_re = cr_ref[0]
    c_im = ci_ref[0]

    e = etab_ref[...]
    pm = jnp.exp(lr * dt * e)
    ph = li * dt * e
    pr = pm * jnp.cos(ph)
    pi = pm * jnp.sin(ph)

    def rows(i, sm_r, sm_i):
        a_r = jnp.broadcast_to(pr[i * L:(i + 1) * L][:, None, :], (L, H, P)).reshape(L * H, P)
        a_i = jnp.broadcast_to(pi[i * L:(i + 1) * L][:, None, :], (L, H, P)).reshape(L * H, P)
        s_r = jnp.broadcast_to(sm_r[None], (L, H, P)).reshape(L * H, P)
        s_i = jnp.broadcast_to(sm_i[None], (L, H, P)).reshape(L * H, P)
        return a_r * s_r - a_i * s_i, a_r * s_i + a_i * s_r

    def real_outer(i, j, xr, yr):
        x_re, x_im = rows(i, bb_re, bb_im)
        y_re, y_im = rows(j, c_re, c_im)
        return _dot_nt(x_re[xr], y_re[yr], HIGHEST) - _dot_nt(x_im[xr], y_im[yr], HIGHEST)

    mm = real_outer(0, 0, slice(0, H), slice(0, H))
    for lvl, s in reversed(list(enumerate(S5_LEVELS))):
        n = s * H
        off = real_outer(1 + lvl, 1 + len(S5_LEVELS) + lvl, slice(0, n), slice(n, 2 * n))
        zero = jnp.zeros((n, n), F32)
        mm = jnp.concatenate([jnp.concatenate([mm, off], axis=1), jnp.concatenate([zero, mm], axis=1)], axis=0)
    ri = lax.broadcasted_iota(jnp.int32, (L * H, L * H), 0)
    ci = lax.broadcasted_iota(jnp.int32, (L * H, L * H), 1)
    mm = mm + jnp.where(ri == ci, dsk_ref[0], 0.0)
    z_re, z_im = rows(1 + 2 * len(S5_LEVELS), bb_re, bb_im)
    w_re, w_im = rows(2 + 2 * len(S5_LEVELS), c_re, c_im)
    m_ref[0] = mm.astype(BF16)
    bzr_ref[0] = z_re.astype(BF16)
    bzi_ref[0] = z_im.astype(BF16)
    cyr_ref[0] = w_re.astype(BF16)
    cyi_ref[0] = (-w_im).astype(BF16)
    m16 = jnp.exp(lr * dt * L)
    ar_ref[0] = m16 * jnp.cos(li * dt * L)
    ai_ref[0] = m16 * jnp.sin(li * dt * L)


def _s5_prep(lam_re, lam_im, log_dt, b_re, b_im, c_re, c_im, d_skip):
    g, p, h = b_re.shape
    lh = S5_CHUNK * h
    g3 = lambda i: (i, 0, 0)
    ins = [lam_re.reshape(g, 1, p), lam_im.reshape(g, 1, p),
           jnp.broadcast_to(log_dt.reshape(g, 1, 1), (g, 1, p)),
           b_re.transpose(0, 2, 1), b_im.transpose(0, 2, 1), c_re, c_im,
           jnp.tile(d_skip, (1, S5_CHUNK)).reshape(g, 1, lh)]
    in_specs = [pl.BlockSpec((1,) + a.shape[1:], g3) for a in ins]
    etab = _s5_exponents()
    ins.append(etab)
    in_specs.append(_const_spec(etab.shape))
    out_shape = [jax.ShapeDtypeStruct((g, lh, lh), BF16)] + \
                [jax.ShapeDtypeStruct((g, lh, p), BF16)] * 4 + \
                [jax.ShapeDtypeStruct((g, 1, p), F32)] * 2
    out_specs = [pl.BlockSpec((1,) + s.shape[1:], g3) for s in out_shape]
    return pl.pallas_call(
        _s5prep_kernel, grid=(g,), in_specs=in_specs, out_specs=out_specs, out_shape=out_shape,
        compiler_params=_params(("arbitrary",)), name="s5_prep",
    )(*ins)


def _s5_kernel(u_ref, m_ref, bzr_ref, bzi_ref, cyr_ref, cyi_ref, ar_ref, ai_ref, y_ref,
               zr_sc, zi_sc, xr_sc, xi_sc, *, nch):
    L = S5_CHUNK
    xcat = jnp.concatenate([u_ref[k, 0] for k in range(L)], axis=1)
    zr_sc[...] = _dot(xcat, bzr_ref[0])
    zi_sc[...] = _dot(xcat, bzi_ref[0])
    a_r = ar_ref[0]
    a_i = ai_ref[0]

    def body(c, carry):
        x_r, x_i = carry
        xr_sc[pl.ds(c, 1), :] = x_r
        xi_sc[pl.ds(c, 1), :] = x_i
        z_r = zr_sc[pl.ds(c, 1), :]
        z_i = zi_sc[pl.ds(c, 1), :]
        return a_r * x_r - a_i * x_i + z_r, a_r * x_i + a_i * x_r + z_i

    zero = jnp.zeros((1, zr_sc.shape[1]), F32)
    lax.fori_loop(0, nch, body, (zero, zero))
    x_r = xr_sc[...].astype(BF16)
    x_i = xi_sc[...].astype(BF16)
    tw = 2 * LANES
    for n in range(L // 2):
        kk = tw * (n + 1)
        cols = slice(n * tw, (n + 1) * tw)
        y = (_dot(xcat[:, :kk], m_ref[0, :kk, cols]) + _dot(x_r, cyr_ref[0, :, cols])
             + _dot(x_i, cyi_ref[0, :, cols]))
        y_ref[2 * n, 0] = y[:, :LANES].astype(BF16)
        y_ref[2 * n + 1, 0] = y[:, LANES:].astype(BF16)


def _s5_block_kernel(m_ref, bzr_ref, bzi_ref, cyr_ref, cyi_ref, m2_ref, bzr2_ref, bzi2_ref, cyr2_ref, cyi2_ref):
    L, H, P = S5_CHUNK, SSM_GROUP, SSM_STATE
    gb = LANES // H
    w = L * LANES
    r = lax.broadcasted_iota(jnp.int32, (L * H, w), 0)
    q = lax.broadcasted_iota(jnp.int32, (L * H, w), 1)
    same_tc = ((r >> 4) == (q >> 7)) & ((r & (H - 1)) == (q & (H - 1)))
    q_grp = (q >> 4) & (gb - 1)
    sr = lax.broadcasted_iota(jnp.int32, (P, gb * P), 0)
    sq = lax.broadcasted_iota(jnp.int32, (P, gb * P), 1)
    eye_p = jnp.where(lax.broadcasted_iota(jnp.int32, (P, P), 0) == lax.broadcasted_iota(jnp.int32, (P, P), 1),
                      1.0, 0.0).astype(BF16)
    for g in range(gb):
        spread_c = jnp.where(same_tc & (q_grp == g), 1.0, 0.0).astype(BF16)
        spread_s = jnp.where(sq - g * P == sr, 1.0, 0.0).astype(BF16)
        xm = _dot(m_ref[g], spread_c).astype(BF16)
        xr = _dot(bzr_ref[g], spread_s).astype(BF16)
        xi = _dot(bzi_ref[g], spread_s).astype(BF16)
        for k in range(L):
            rows = slice(k * LANES + g * H, k * LANES + (g + 1) * H)
            m2_ref[0, rows, :] = xm[k * H:(k + 1) * H, :]
            bzr2_ref[0, rows, :] = xr[k * H:(k + 1) * H, :]
            bzi2_ref[0, rows, :] = xi[k * H:(k + 1) * H, :]
        for src, dst in ((cyr_ref, cyr2_ref), (cyi_ref, cyi2_ref)):
            cy_t = _dot_nt(eye_p, src[g]).astype(BF16)
            dst[0, g * P:(g + 1) * P, :] = _dot(cy_t, spread_c).astype(BF16)


def _s5_block_operators(ops, gb):
    m, bzr, bzi, cyr, cyi, a_r, a_i = ops
    g, lh, p = bzr.shape
    nb = g // gb
    w = S5_CHUNK * LANES
    sw = gb * p
    g3 = lambda j: (j, 0, 0)
    out_shape = [jax.ShapeDtypeStruct((nb, w, w), BF16), jax.ShapeDtypeStruct((nb, w, sw), BF16),
                 jax.ShapeDtypeStruct((nb, w, sw), BF16), jax.ShapeDtypeStruct((nb, sw, w), BF16),
                 jax.ShapeDtypeStruct((nb, sw, w), BF16)]
    m2, bzr2, bzi2, cyr2, cyi2 = pl.pallas_call(
        _s5_block_kernel,
        grid=(nb,),
        in_specs=[pl.BlockSpec((gb, lh, lh), g3)] + [pl.BlockSpec((gb, lh, p), g3)] * 4,
        out_specs=[pl.BlockSpec((1,) + s.shape[1:], g3) for s in out_shape],
        out_shape=out_shape,
        compiler_params=_params(("arbitrary",)),
        name="s5_block_ops",
    )(m, bzr, bzi, cyr, cyi)
    return m2, bzr2, bzi2, cyr2, cyi2, a_r.reshape(nb, 1, sw), a_i.reshape(nb, 1, sw)


def _s5_scan(u_k, ops, bsz):
    L, nb, rows, _ = u_k.shape
    gb = LANES // SSM_GROUP
    m2, bzr, bzi, cyr, cyi, a_r, a_i = _s5_block_operators(ops, gb)
    nch = rows // bsz
    w = L * LANES
    sw = gb * SSM_STATE
    op3 = lambda j, b: (j, 0, 0)
    xspec = pl.BlockSpec((L, 1, nch, LANES), lambda j, b: (0, j, b, 0))
    in_specs = [xspec, pl.BlockSpec((1, w, w), op3),
                pl.BlockSpec((1, w, sw), op3), pl.BlockSpec((1, w, sw), op3),
                pl.BlockSpec((1, sw, w), op3), pl.BlockSpec((1, sw, w), op3),
                pl.BlockSpec((1, 1, sw), op3), pl.BlockSpec((1, 1, sw), op3)]
    return pl.pallas_call(
        functools.partial(_s5_kernel, nch=nch),
        grid=(nb, bsz),
        in_specs=in_specs,
        out_specs=xspec,
        out_shape=jax.ShapeDtypeStruct(u_k.shape, BF16),
        scratch_shapes=[pltpu.VMEM((nch, sw), F32)] * 4,
        compiler_params=_params(("arbitrary", "arbitrary")),
        name="s5_scan",
    )(u_k, m2, bzr, bzi, cyr, cyi, a_r, a_i)


def _cmp_kernel(tok_ref, w1x_ref, pe_ref, w1_ref, w2x_ref, o_ref, *, transpose_out):
    nh = N_KV_HEADS * CMP_HIDDEN
    g = _dot(tok_ref[...], w1x_ref[...])
    rows = g.shape[0]
    bias = jnp.dot(jnp.broadcast_to(pe_ref[...], (8, pe_ref.shape[1])), w1_ref[...],
                   preferred_element_type=F32, precision=HIGHEST)[0:1]
    bias = jnp.concatenate([bias] * N_KV_HEADS, axis=1)
    hid = g[:, :nh] + pltpu.roll(g[:, nh:], rows - 1, 0) + bias
    out = _dot(jax.nn.gelu(hid).astype(BF16), w2x_ref[...])
    if transpose_out:
        o_ref[0] = out.T.astype(BF16)
    else:
        for hh in range(N_KV_HEADS):
            o_ref[0, hh] = out[:, hh * HEAD_DIM:(hh + 1) * HEAD_DIM].astype(BF16)


def _compress(tok, pe, w1, w2, bsz, seq, transpose_out):
    ncp = seq // CMP_STRIDE
    half = CMP_STRIDE * HEAD_DIM
    eye = jnp.eye(N_KV_HEADS, dtype=F32)
    w1r = w1.reshape(2, CMP_STRIDE, HEAD_DIM, CMP_HIDDEN)
    w1x = jnp.einsum('zkdj,hg->khdzgj', w1r, eye).reshape(CMP_STRIDE * KV_DIM, 2 * N_KV_HEADS * CMP_HIDDEN)
    w2x = jnp.einsum('jd,hg->hjgd', w2, eye).reshape(N_KV_HEADS * CMP_HIDDEN, KV_DIM)
    tok2 = tok.reshape(bsz * ncp, CMP_STRIDE * KV_DIM)
    if transpose_out:
        out_shape = jax.ShapeDtypeStruct((bsz, KV_DIM, ncp), BF16)
        out_spec = pl.BlockSpec((1, KV_DIM, ncp), lambda b: (b, 0, 0))
    else:
        out_shape = jax.ShapeDtypeStruct((bsz, N_KV_HEADS, ncp, HEAD_DIM), BF16)
        out_spec = pl.BlockSpec((1, N_KV_HEADS, ncp, HEAD_DIM), lambda b: (b, 0, 0, 0))
    return pl.pallas_call(
        functools.partial(_cmp_kernel, transpose_out=transpose_out),
        grid=(bsz,),
        in_specs=[pl.BlockSpec((ncp, CMP_STRIDE * KV_DIM), lambda b: (b, 0)),
                  _const_spec(w1x.shape), _const_spec((1, 2 * half)), _const_spec(w1.shape),
                  _const_spec(w2x.shape)],
        out_specs=out_spec, out_shape=out_shape,
        compiler_params=_params(("arbitrary",)),
        name="compress_k" if transpose_out else "compress_v",
    )(tok2, w1x.astype(BF16), pe.reshape(1, 2 * half), w1, w2x.astype(BF16))


def _nsa_kernel(q_ref, kct_ref, vc_ref, kst_ref, vs_ref, kwt_ref, vw_ref, gt_ref, amat_ref, o_ref,
                rhs_sc, lhs_sc, m_sc, st_sc, mw_sc, stw_sc, *, tq, seq):
    nblk = seq // SEL_BLOCK
    ncp = seq // CMP_STRIDE
    rows = GQA * tq
    nslab = tq // LANES
    sub = min(ATT_SUB_ROWS, tq)
    qi = pl.program_id(2)
    t0 = qi * tq

    @pl.when(qi == 0)
    def _():
        rhs_sc[0:HEAD_DIM, :] = kst_ref[0]
        rhs_sc[HEAD_DIM:LANES, :] = jnp.zeros((LANES - HEAD_DIM, seq), BF16)
        cw = min(seq, 1024)

        def fill(c, _):
            col = c * cw + lax.broadcasted_iota(jnp.int32, (nblk, cw), 1)
            blk = lax.broadcasted_iota(jnp.int32, (nblk, cw), 0)
            onehot = jnp.where((col >> 6) == blk, 1.0, 0.0).astype(BF16)
            rhs_sc[LANES:LANES + nblk, pl.ds(pl.multiple_of(c * cw, cw), cw)] = onehot
            return 0

        lax.fori_loop(0, seq // cw, fill, 0)

    t_q = t0 + lax.broadcasted_iota(jnp.int32, (tq, 1), 0)

    def reset(state):
        state[0][...] = jnp.full((rows, LANES), NEG_BIG, F32)
        state[1][...] = jnp.zeros((rows, LANES), F32)

    def online(state, g, s, v_aug, h=None):
        m_ref, st_ref = state
        sl = slice(g * tq, (g + 1) * tq) if h is None else slice(g * tq + h * sub, g * tq + (h + 1) * sub)
        m_old = m_ref[sl, :]
        mx = s[:, 0:LANES]
        for j in range(1, nslab):
            mx = jnp.maximum(mx, s[:, j * LANES:(j + 1) * LANES])
        m_new = jnp.maximum(m_old, jnp.max(mx, axis=-1, keepdims=True))
        alpha = jnp.exp2(m_old - m_new)
        p = jnp.concatenate([jnp.exp2(s[:, j * LANES:(j + 1) * LANES] - m_new) for j in range(nslab)], axis=1)
        st_ref[sl, :] = alpha * st_ref[sl, :] + _dot(p.astype(BF16), v_aug)
        m_ref[sl, :] = m_new

    def result(state, g):
        st = state[1][g * tq:(g + 1) * tq, :]
        return st[:, 0:HEAD_DIM] * (1.0 / st[:, HEAD_DIM:HEAD_DIM + 1])

    def kpos(kj):
        return kj * tq + lax.broadcasted_iota(jnp.int32, (1, tq), 1)

    win_st = (mw_sc, stw_sc)
    reset(win_st)
    for w in range(WINDOW // tq + 1):
        kj = qi - WINDOW // tq + w
        off = pl.multiple_of(jnp.maximum(kj, 0) * tq, tq)
        kt = kwt_ref[0, :, pl.ds(off, tq)]
        v_aug = vw_ref[0, 0, pl.ds(off, tq), :]
        kp = kpos(kj)
        if w == WINDOW // tq:
            mask = kp <= t_q
        elif w == 0:
            mask = (kp > t_q - WINDOW) & (kp >= 0)
        else:
            mask = jnp.broadcast_to(kp >= 0, (tq, tq))
        for g in range(GQA):
            online(win_st, g, jnp.where(mask, _dot(q_ref[0, g], kt), NEG_BIG), v_aug)

    n_col = lax.broadcasted_iota(jnp.int32, (1, ncp), 1)
    valid = (n_col * CMP_STRIDE + (CMP_BLOCK - 1)) <= t_q
    row_live = t_q >= CMP_BLOCK - 1
    kct = kct_ref[0]
    vc = vc_ref[0, 0]
    o_cmp = []
    psum = None
    for g in range(GQA):
        sm = jnp.where(valid, _dot(q_ref[0, g], kct), NEG_BIG)
        e = jnp.exp2(sm - jnp.max(sm, axis=-1, keepdims=True))
        p = e * jnp.where(row_live, 1.0 / jnp.sum(e, axis=-1, keepdims=True), 0.0)
        o_cmp.append(_dot(p.astype(BF16), vc))
        psum = p if g == 0 else psum + p

    amat = amat_ref[...]
    p_hi = psum.astype(BF16)
    r1 = psum - p_hi.astype(F32)
    p_mid = r1.astype(BF16)
    p_lo = (r1 - p_mid.astype(F32)).astype(BF16)
    imp = _dot(p_hi, amat) + _dot(p_mid, amat) + _dot(p_lo, amat)
    blk = lax.broadcasted_iota(jnp.int32, (nblk, 1), 0)
    t_l = t0 + lax.broadcasted_iota(jnp.int32, (1, tq), 1)
    cur = t_l >> 6
    forced = (blk == 0) | (blk == cur) | (blk == cur - 1)
    impm = jnp.where(forced, -jnp.inf, jnp.where(blk * SEL_BLOCK <= t_l, imp.T, NEG_BIG))
    impm = impm.reshape(nblk // 8, 8, tq)
    n_forced = 3
    blk_f = (lax.broadcasted_iota(jnp.int32, (nblk // 8, 8, tq), 0) * 8
             + lax.broadcasted_iota(jnp.int32, (nblk // 8, 8, tq), 1)).astype(F32)

    def over_blocks(x, op):
        r = x[0]
        for a in range(1, nblk // 8):
            r = op(r, x[a])
        for sh in (4, 2, 1):
            r = op(r, pltpu.roll(r, sh, 0))
        return r[None]

    selneg = jnp.where(forced, 0.0, NEG_BIG).reshape(nblk // 8, 8, tq)
    for _ in range(max(min(N_SELECT, nblk) - n_forced, 0)):
        best = over_blocks(impm, jnp.maximum)
        idx = over_blocks(jnp.where(impm == best, blk_f, float(nblk)), jnp.minimum)
        pick = blk_f == idx
        selneg = jnp.where(pick, 0.0, selneg)
        impm = jnp.where(pick, -jnp.inf, impm)
    selneg = selneg.reshape(nblk, tq).T.astype(BF16)

    for g in range(GQA):
        sl = slice(g * tq, (g + 1) * tq)
        lhs_sc[sl, 0:HEAD_DIM] = q_ref[0, g]
        lhs_sc[sl, HEAD_DIM:LANES] = jnp.zeros((tq, LANES - HEAD_DIM), BF16)
        lhs_sc[sl, LANES:LANES + nblk] = selneg

    sel_st = (m_sc, st_sc)
    reset(sel_st)

    def sel_scores(kj):
        rhs = rhs_sc[:, pl.ds(pl.multiple_of(kj * tq, tq), tq)]
        return tuple(_dot(lhs_sc[g * tq + h * sub:g * tq + (h + 1) * sub, :], rhs)
                     for g in range(GQA) for h in range(tq // sub))

    def sel_update(kj, scores, mask):
        v_aug = vs_ref[0, 0, pl.ds(pl.multiple_of(kj * tq, tq), tq), :]
        for g in range(GQA):
            for h in range(tq // sub):
                sc = scores[g * (tq // sub) + h]
                if mask is not None:
                    sc = jnp.where(mask[h * sub:(h + 1) * sub], sc, NEG_BIG)
                online(sel_st, g, sc, v_aug, h)

    def sel_step(kj, scores):
        nxt = sel_scores(kj + 1)
        sel_update(kj, scores, None)
        return nxt

    scores = lax.fori_loop(0, qi, sel_step, sel_scores(0))
    sel_update(qi, scores, kpos(qi) <= t_q)

    gt = gt_ref[0]
    for g in range(GQA):
        o = (gt[:, 3 * g:3 * g + 1] * o_cmp[g] + gt[:, 3 * g + 1:3 * g + 2] * result(sel_st, g)
             + gt[:, 3 * g + 2:3 * g + 3] * result(win_st, g))
        o_ref[:, g * HEAD_DIM:(g + 1) * HEAD_DIM] = o


def _importance_matrix(seq):
    ncp = seq // CMP_STRIDE
    nblk = seq // SEL_BLOCK
    ratio = SEL_BLOCK // CMP_STRIDE
    n = jnp.arange(ncp)[:, None]
    j = jnp.arange(nblk)[None, :]
    a = sum(((n == ratio * j + r).astype(F32) + (n == ratio * j + r + 1).astype(F32)) for r in range(ratio))
    return jnp.where(n < ncp - 1, a, 0.0).astype(BF16)


def _nsa_attention(q_hm, kct, vc_hm, kst, vs_hm, kwt, vw_hm, gates, bsz, seq):
    tq = min(ATT_TILE, seq)
    nq = seq // tq
    nblk = seq // SEL_BLOCK
    ncp = seq // CMP_STRIDE
    rows = GQA * tq
    amat = _importance_matrix(seq)
    head = lambda b, h, i: (b, h, 0, 0)
    headt = lambda b, h, i: (b, h, 0)
    in_specs = [pl.BlockSpec((1, GQA, tq, HEAD_DIM), lambda b, h, i: (b, h, i, 0)),
                pl.BlockSpec((1, HEAD_DIM, ncp), headt),
                pl.BlockSpec((1, 1, ncp, HEAD_DIM), head),
                pl.BlockSpec((1, HEAD_DIM, seq), headt),
                pl.BlockSpec((1, 1, seq, LANES), head),
                pl.BlockSpec((1, HEAD_DIM, seq), headt),
                pl.BlockSpec((1, 1, seq, LANES), head),
                pl.BlockSpec((1, tq, LANES), lambda b, h, i: (h, b * nq + i, 0)),
                _const_spec(amat.shape)]
    return pl.pallas_call(
        functools.partial(_nsa_kernel, tq=tq, seq=seq),
        grid=(bsz, N_KV_HEADS, nq),
        in_specs=in_specs,
        out_specs=pl.BlockSpec((tq, GQA * HEAD_DIM), lambda b, h, i: (b * nq + i, h)),
        out_shape=jax.ShapeDtypeStruct((bsz * seq, N_HEADS * HEAD_DIM), F32),
        scratch_shapes=[pltpu.VMEM((LANES + nblk, seq), BF16),
                        pltpu.VMEM((rows, LANES + nblk), BF16),
                        pltpu.VMEM((rows, LANES), F32), pltpu.VMEM((rows, LANES), F32),
                        pltpu.VMEM((rows, LANES), F32), pltpu.VMEM((rows, LANES), F32)],
        compiler_params=_params(("arbitrary", "arbitrary", "arbitrary")),
        name="nsa_attention",
    )(q_hm, kct, vc_hm, kst, vs_hm, kwt, vw_hm, gates, amat)


def _out_kernel(ys_ref, ya_ref, x_ref, wglu_ref, bglu_ref, gs_ref, ga_ref, woa_ref, wob_ref,
                ga1_ref, gf_ref, sh2_ref, sc2_ref, x1_ref, h2_ref, ysc):
    nchunk = ysc.shape[1] // S5_CHUNK
    for j in range(ysc.shape[0]):
        for k in range(S5_CHUNK):
            ysc[j, pl.ds(k, nchunk, stride=S5_CHUNK), :] = ys_ref[k, j].astype(F32)
    z = jax.nn.gelu(jnp.concatenate([ysc[j] for j in range(ysc.shape[0])], axis=1))
    glu = z * jax.nn.sigmoid(_dot(z.astype(BF16), wglu_ref[...]) + bglu_ref[...])
    n1 = _rms(glu, gs_ref[...]).astype(BF16)
    n2 = _rms(ya_ref[...], ga_ref[...]).astype(BF16)
    y = _dot(n1, woa_ref[...]) + _dot(n2, wob_ref[...])
    x1 = x_ref[...] + ga1_ref[0] * y
    x1_ref[...] = x1
    h2_ref[...] = (_rms(x1, gf_ref[...]) * (1.0 + sc2_ref[0]) + sh2_ref[0]).astype(BF16)


def _out_proj(y_ssm, y_att, x2, w_glu, b_glu, g_ssm, g_nsa, w_out, ga1, g_ffn, sh2, sc2, bsz, seq, tm):
    t, d = x2.shape
    d_att = y_att.shape[1]
    d_ssm = d - d_att
    nt = seq // tm
    row = lambda i: (i, 0)
    bat = lambda i: (i // nt, 0, 0)
    w_out = w_out.astype(BF16)
    in_specs = [pl.BlockSpec((S5_CHUNK, d_ssm // LANES, tm // S5_CHUNK, LANES), lambda i: (0, 0, i, 0)),
                pl.BlockSpec((tm, d_att), row), pl.BlockSpec((tm, d), row),
                _const_spec((d_ssm, d_ssm)), _const_spec((1, d_ssm)), _const_spec((1, d_ssm)),
                _const_spec((1, d_att)), _const_spec((d_ssm, d)), _const_spec((d_att, d)),
                pl.BlockSpec((1, 1, d), bat), _const_spec((1, d)),
                pl.BlockSpec((1, 1, d), bat), pl.BlockSpec((1, 1, d), bat)]
    return pl.pallas_call(
        _out_kernel,
        grid=(t // tm,),
        in_specs=in_specs,
        out_specs=[pl.BlockSpec((tm, d), row), pl.BlockSpec((tm, d), row)],
        out_shape=[jax.ShapeDtypeStruct((t, d), F32), jax.ShapeDtypeStruct((t, d), BF16)],
        scratch_shapes=[pltpu.VMEM((d_ssm // LANES, tm, LANES), F32)],
        compiler_params=_params(("arbitrary",)),
        name="out_proj",
    )(y_ssm, y_att, x2, w_glu.astype(BF16), b_glu.reshape(1, d_ssm), g_ssm.reshape(1, d_ssm),
      g_nsa.reshape(1, d_att), w_out[:d_ssm], w_out[d_ssm:], ga1.reshape(bsz, 1, d), g_ffn.reshape(1, d),
      sh2.reshape(bsz, 1, d), sc2.reshape(bsz, 1, d))


FFN_HALO = 16
FFN_COL = 256


def _ffn_kernel(h_ref, halo_ref, x1_ref, wv_ref, wg_ref, cwv_ref, cwg_ref, cbv_ref, cbg_ref, wd_ref,
                ga2_ref, gfin_ref, o_ref, up_a, up_b, acc_sc, *, nt, nj, nsteps):
    s = pl.program_id(0)
    su = jnp.minimum(s, nsteps - 1)
    sd = jnp.maximum(s - 1, 0)
    iu = lax.div(su, nj)
    jd = lax.rem(sd, nj)

    @pl.when(s == 0)
    def _():
        up_a[...] = jnp.zeros(up_a.shape, F32)
        up_b[...] = jnp.zeros(up_b.shape, F32)
        acc_sc[...] = jnp.zeros(acc_sc.shape, F32)

    def step(up_prev, up_next):
        def conv(i, cw_ref, cb_ref, cols):
            up = up_prev[i, :, cols]
            cw = cw_ref[:, cols]
            c = (pltpu.roll(up, 2, 0) * cw[0:1] + pltpu.roll(up, 1, 0) * cw[1:2] + up * cw[2:3])
            return c[FFN_HALO:] + cb_ref[:, cols]

        halo = halo_ref[...]
        halo = jnp.where(lax.rem(iu, nt) == 0, jnp.zeros_like(halo), halo)
        hx = jnp.concatenate([halo, h_ref[...]], axis=0)

        part = None
        for c0 in range(0, wv_ref.shape[1], FFN_COL):
            cols = slice(c0, c0 + FFN_COL)
            up_next[0, :, cols] = _dot(hx, wv_ref[:, cols])
            val = conv(0, cwv_ref, cbv_ref, cols)
            gate = conv(1, cwg_ref, cbg_ref, cols)
            act = (gate * jax.nn.sigmoid(gate) * val).astype(BF16)
            up_next[1, :, cols] = _dot(hx, wg_ref[:, cols])
            d = _dot(act, wd_ref[cols, :])
            part = d if part is None else part + d
        acc_sc[...] = jnp.where(jd > 0, acc_sc[...], 0.0) + part

    odd = lax.rem(s, 2)

    @pl.when(odd == 0)
    def _():
        step(up_b, up_a)

    @pl.when(odd == 1)
    def _():
        step(up_a, up_b)

    @pl.when((s > 0) & (jd == nj - 1))
    def _():
        x2 = x1_ref[...] + ga2_ref[0] * acc_sc[...]
        o_ref[...] = _rms(x2, gfin_ref[...])


def _conv_ffn(h2, x1, w_up, conv_w, conv_b, w_down, ga2, g_final, bsz, seq, tm, tn):
    t, d = x1.shape
    dff = w_down.shape[0]
    nt = seq // tm
    nj = dff // tn
    nsteps = (t // tm) * nj
    w_up = w_up.astype(BF16)
    w_down = w_down.astype(BF16)
    cb = conv_b.reshape(1, 2 * dff)
    hb = tm // FFN_HALO
    iu = lambda s: lax.div(jnp.minimum(s, nsteps - 1), nj)
    ju = lambda s: lax.rem(jnp.minimum(s, nsteps - 1), nj)
    idn = lambda s: lax.div(jnp.maximum(s - 1, 0), nj)
    jdn = lambda s: lax.rem(jnp.maximum(s - 1, 0), nj)
    in_specs = [pl.BlockSpec((tm, d), lambda s: (iu(s), 0)),
                pl.BlockSpec((FFN_HALO, d), lambda s: (jnp.maximum(iu(s) * hb - 1, 0), 0)),
                pl.BlockSpec((tm, d), lambda s: (idn(s), 0)),
                pl.BlockSpec((d, tn), lambda s: (0, ju(s))),
                pl.BlockSpec((d, tn), lambda s: (0, nj + ju(s))),
                pl.BlockSpec((CONV_WIDTH, tn), lambda s: (0, jdn(s))),
                pl.BlockSpec((CONV_WIDTH, tn), lambda s: (0, nj + jdn(s))),
                pl.BlockSpec((1, tn), lambda s: (0, jdn(s))),
                pl.BlockSpec((1, tn), lambda s: (0, nj + jdn(s))),
                pl.BlockSpec((tn, d), lambda s: (jdn(s), 0)),
                pl.BlockSpec((1, 1, d), lambda s: (lax.div(idn(s), nt), 0, 0)),
                pl.BlockSpec((1, d), lambda s: (0, 0))]
    return pl.pallas_call(
        functools.partial(_ffn_kernel, nt=nt, nj=nj, nsteps=nsteps),
        grid=(nsteps + 1,),
        in_specs=in_specs,
        out_specs=pl.BlockSpec((tm, d), lambda s: (idn(s), 0)),
        out_shape=jax.ShapeDtypeStruct((t, d), F32),
        scratch_shapes=[pltpu.VMEM((2, FFN_HALO + tm, tn), F32), pltpu.VMEM((2, FFN_HALO + tm, tn), F32),
                        pltpu.VMEM((tm, d), F32)],
        compiler_params=_params(("arbitrary",)),
        name="conv_ffn",
    )(h2, h2, x1, w_up, w_up, conv_w, conv_w, cb, cb, w_down, ga2.reshape(bsz, 1, d), g_final.reshape(1, d))


def _block(x, mod, g_mix_norm, w_in, s5_params, cmp_k, cmp_v, g_ssm_out, g_nsa_out, w_glu, b_glu, w_out,
           g_ffn_norm, w_up, conv_w, conv_b, w_down, g_final):
    bsz, seq, d = x.shape
    t = bsz * seq
    d_ssm = d // 2
    x2 = x.reshape(t, d)
    sh1, sc1, ga1, sh2, sc2, ga2 = jnp.split(mod, 6, axis=-1)
    tm = min(512, seq)

    u, q_hm, kc_tok, vc_tok, kst, vs_hm, kwt, vw_hm, gates = _input_proj(
        x2, g_mix_norm, sh1, sc1, w_in, _rope_tables(seq), bsz, seq, tm)

    y_ssm = _s5_scan(u, _s5_prep(*s5_params), bsz)

    kct = _compress(kc_tok, *cmp_k, bsz, seq, True)
    vc_hm = _compress(vc_tok, *cmp_v, bsz, seq, False)
    y_att = _nsa_attention(q_hm, kct, vc_hm, kst, vs_hm, kwt, vw_hm, gates, bsz, seq)

    x1, h2 = _out_proj(y_ssm, y_att, x2, w_glu, b_glu, g_ssm_out, g_nsa_out, w_out, ga1, g_ffn_norm,
                       sh2, sc2, bsz, seq, tm)
    out = _conv_ffn(h2, x1, w_up, conv_w, conv_b, w_down, ga2, g_final, bsz, seq, tm, 512)
    return out.reshape(bsz, seq, d)


def kernel(x, c, w_ada, b_ada, g_mix_norm, w_in, lam_re, lam_im, log_dt, b_re, b_im, c_re, c_im, d_skip,
           w_glu, b_glu, pe_k, w1_k, w2_k, pe_v, w1_v, w2_v, g_ssm_out, g_nsa_out, w_out, g_ffn_norm,
           w_up, conv_w, conv_b, w_down, g_final):
    l = 0
    mod = _adaln_mod(c, w_ada[l], b_ada[l])
    return _block(x, mod, g_mix_norm[l], w_in[l],
                  (lam_re[l], lam_im[l], log_dt[l], b_re[l], b_im[l], c_re[l], c_im[l], d_skip[l]),
                  (pe_k[l], w1_k[l], w2_k[l]), (pe_v[l], w1_v[l], w2_v[l]),
                  g_ssm_out[l], g_nsa_out[l], w_glu[l], b_glu[l], w_out[l], g_ffn_norm[l],
                  w_up[l], conv_w[l], conv_b[l], w_down[l], g_final)
```
